```python
import math
import jax, jax.numpy as jnp
from jax import lax
import numpy as np

D_MODEL = 4096
BATCH = 4
SEQ = 2048
DEPTH = 2
DEC_BATCH = 8
DEC_SEQ = 4
PAST_LEN = 16384
PAGE_SIZE = 128

D_CONV = D_MODEL
CONV_WIDTH = 3
HEAD_DIM = 128
N_HEADS = D_MODEL // HEAD_DIM
ATTN_WIDTH = N_HEADS * HEAD_DIM
GROUPS = ((128, 1), (512, 4), (2048, 16))
N_GROUPS = len(GROUPS)
N_REL = 128
BAND = 128
N_BUCKETS = 32
MAX_DISTANCE = 2048
D_FF = 11008
RMS_EPS = 1e-6
NEG = -1e30
ATTN_SCALE = 1.0 / math.sqrt(HEAD_DIM)
GATE_BASE = 3 * D_CONV + 3 * N_GROUPS * ATTN_WIDTH
IN_WIDTH = GATE_BASE + 2 * D_MODEL

kernel_name = "gated_conv_dilated_swa_macaron_decoder_step"


def rmsnorm(x, g):
    xf = x.astype(jnp.float32)
    r = lax.rsqrt(jnp.mean(xf * xf, axis=-1, keepdims=True) + RMS_EPS)
    return (xf * r).astype(x.dtype) * g.astype(x.dtype)


def swiglu(h, w_gate, w_up, w_down):
    return (jax.nn.silu(h @ w_gate) * (h @ w_up)) @ w_down


def t5_bucket(dist):
    dist = np.asarray(dist)
    max_exact = N_BUCKETS // 2
    large = max_exact + (np.log(np.maximum(dist, max_exact) / max_exact)
                         / np.log(MAX_DISTANCE / max_exact)
                         * (N_BUCKETS - max_exact)).astype(np.int32)
    large = np.minimum(large, N_BUCKETS - 1)
    return np.where(dist < max_exact, dist, large).astype(np.int32)


def group_bias(rel_bias, g, dil):
    buckets = t5_bucket(np.arange(N_REL + 1) * dil)
    return rel_bias[buckets][:, g * N_HEADS:(g + 1) * N_HEADS].T.astype(jnp.float32)


def band_attend(q, k, v, bias):
    N, n, H, Dh = q.shape
    nb = -(-n // BAND)
    pad_end = nb * BAND - n
    qb = jnp.pad(q, ((0, 0), (0, pad_end), (0, 0), (0, 0))).reshape(N, nb, BAND, H, Dh)

    def band(t):
        tp = jnp.pad(t, ((0, 0), (BAND, pad_end), (0, 0), (0, 0))).reshape(N, nb + 1, BAND, H, Dh)
        return jnp.concatenate([tp[:, :-1], tp[:, 1:]], axis=2)

    kb, vb = band(k), band(v)
    a = np.arange(BAND)[:, None]
    c = np.arange(2 * BAND)[None, :]
    delta = a + BAND - c
    in_window = (delta >= 0) & (delta <= N_REL)
    real_key = (np.arange(nb)[:, None] * BAND + np.arange(2 * BAND)[None, :]) >= BAND
    mask = in_window[None] & real_key[:, None, :]
    bias_band = bias[:, np.clip(delta, 0, N_REL)]
    logits = jnp.einsum('nbqhd,nbkhd->nbhqk', qb, kb,
                        preferred_element_type=jnp.float32) * ATTN_SCALE + bias_band[None, None]
    logits = jnp.where(mask[None, :, None], logits, NEG)
    m = jnp.max(logits, axis=-1, keepdims=True)
    p = jnp.exp(logits - m)
    s = jnp.sum(p, axis=-1, keepdims=True)
    o = jnp.einsum('nbhqk,nbkhd->nbqhd', (p / s).astype(v.dtype), vb)
    lse = (m + jnp.log(s))[..., 0].transpose(0, 1, 3, 2)
    o = o.reshape(N, nb * BAND, H, Dh)[:, :n]
    lse = lse.reshape(N, nb * BAND, H)[:, :n]
    return o, lse


def dilated_prompt(q, k, v, bias, dil):
    B_, S_, H, Dh = q.shape
    n = S_ // dil

    def split(t):
        return t.reshape(B_, n, dil, H, Dh).transpose(0, 2, 1, 3, 4).reshape(B_ * dil, n, H, Dh)

    o, lse = band_attend(split(q), split(k), split(v), bias)
    o = o.reshape(B_, dil, n, H, Dh).transpose(0, 2, 1, 3, 4).reshape(B_, S_, H, Dh)
    lse = lse.reshape(B_, dil, n, H).transpose(0, 2, 1, 3).reshape(B_, S_, H)
    return o, lse


def dilated_sample(q, k_new, v_new, kv_buf, bias, dil):
    T = q.shape[1]
    W = kv_buf.shape[1]
    k_all = jnp.concatenate([kv_buf[:, :, 0].astype(k_new.dtype), k_new], axis=1)
    v_all = jnp.concatenate([kv_buf[:, :, 1].astype(v_new.dtype), v_new], axis=1)
    idx = W + np.arange(T)[:, None] - np.arange(N_REL + 1)[None, :] * dil
    valid = idx >= 0
    idx_c = np.maximum(idx, 0)
    kg = k_all[:, idx_c]
    vg = v_all[:, idx_c]
    logits = jnp.einsum('bthd,btjhd->bthj', q, kg,
                        preferred_element_type=jnp.float32) * ATTN_SCALE + bias[None, None]
    logits = jnp.where(valid[None, :, None, :], logits, NEG)
    m = jnp.max(logits, axis=-1, keepdims=True)
    p = jnp.exp(logits - m)
    s = jnp.sum(p, axis=-1, keepdims=True)
    o = jnp.einsum('bthj,btjhd->bthd', (p / s).astype(vg.dtype), vg)
    return o, (m + jnp.log(s))[..., 0]


def short_conv(u, prev, conv_w):
    L = u.shape[1]
    up = jnp.concatenate([prev.astype(u.dtype), u], axis=1)
    y = conv_w[0] * up[:, :L]
    for i in range(1, CONV_WIDTH):
        y = y + conv_w[i] * up[:, i:i + L]
    return y, up[:, L:]


def token_mix(h, w_in, conv_w, w_out, rel_bias, conv_prev, kv_bufs):
    N, L, _ = h.shape
    u = h @ w_in
    b_gate = u[..., :D_CONV]
    c_gate = u[..., D_CONV:2 * D_CONV]
    x_conv = u[..., 2 * D_CONV:3 * D_CONV]
    y_conv, conv_state = short_conv(c_gate * x_conv, conv_prev, conv_w)
    y_conv = b_gate * y_conv

    outs, lses, new_kv = [], [], []
    for g, (window, dil) in enumerate(GROUPS):
        base = 3 * D_CONV + 3 * g * ATTN_WIDTH
        q = u[..., base:base + ATTN_WIDTH].reshape(N, L, N_HEADS, HEAD_DIM)
        k = u[..., base + ATTN_WIDTH:base + 2 * ATTN_WIDTH].reshape(N, L, N_HEADS, HEAD_DIM)
        v = u[..., base + 2 * ATTN_WIDTH:base + 3 * ATTN_WIDTH].reshape(N, L, N_HEADS, HEAD_DIM)
        bias = group_bias(rel_bias, g, dil)
        if kv_bufs is None:
            o, lse = dilated_prompt(q, k, v, bias, dil)
            keep = min(window, L)
            new_kv.append(jnp.stack([k[:, L - keep:], v[:, L - keep:]], axis=2))
        else:
            o, lse = dilated_sample(q, k, v, kv_bufs[g], bias, dil)
            new_kv.append(jnp.stack([k, v], axis=2))
        outs.append(o)
        lses.append(lse)

    alpha = jax.nn.softmax(jnp.stack(lses), axis=0)
    y_attn = jnp.sum(jnp.stack(outs) * alpha[..., None].astype(outs[0].dtype), axis=0)
    y_attn = y_attn.reshape(N, L, ATTN_WIDTH)

    gate_c = jax.nn.sigmoid(u[..., GATE_BASE:GATE_BASE + D_MODEL])
    gate_a = jax.nn.sigmoid(u[..., GATE_BASE + D_MODEL:GATE_BASE + 2 * D_MODEL])
    merged = gate_c * y_conv + gate_a * y_attn
    return merged @ w_out, conv_state, new_kv


def trunk_layer(x, norm_g, w_gate, w_up, w_down, w_in, conv_w, w_out, rel_bias, conv_prev, kv_bufs):
    x = x + 0.5 * rmsnorm(swiglu(rmsnorm(x, norm_g[0]), w_gate[0], w_up[0], w_down[0]), norm_g[1])
    m, conv_state, new_kv = token_mix(rmsnorm(x, norm_g[2]), w_in, conv_w, w_out, rel_bias,
                                      conv_prev, kv_bufs)
    x = x + rmsnorm(m, norm_g[3])
    x = x + 0.5 * rmsnorm(swiglu(rmsnorm(x, norm_g[4]), w_gate[1], w_up[1], w_down[1]), norm_g[5])
    return x, conv_state, new_kv


def setup_inputs(seed: int = 0) -> dict:
    key = jax.random.key(seed)
    ks = jax.random.split(key, 16)
    nrm = jax.random.normal
    f32 = jnp.float32
    x_prompt = nrm(ks[0], (BATCH, SEQ, D_MODEL), f32)
    x_sample = nrm(ks[1], (DEC_BATCH, DEC_SEQ, D_MODEL), f32)
    state_conv = nrm(ks[2], (DEPTH, DEC_BATCH, CONV_WIDTH - 1, D_CONV), f32)
    cache_kv_w128 = nrm(ks[3], (DEPTH, DEC_BATCH, min(GROUPS[0][0], PAST_LEN), 2, N_HEADS, HEAD_DIM), f32)
    cache_kv_w512 = nrm(ks[4], (DEPTH, DEC_BATCH, min(GROUPS[1][0], PAST_LEN), 2, N_HEADS, HEAD_DIM), f32)
    cache_kv_w2048 = nrm(ks[5], (DEPTH, DEC_BATCH, min(GROUPS[2][0], PAST_LEN), 2, N_HEADS, HEAD_DIM), f32)
    rel_bias = 0.5 * nrm(ks[6], (N_BUCKETS, N_GROUPS * N_HEADS), f32)
    norm_g = 1.0 + 0.05 * nrm(ks[7], (DEPTH, 6, D_MODEL), f32)
    ffn_w_gate = nrm(ks[8], (DEPTH, 2, D_MODEL, D_FF), f32) * D_MODEL ** -0.5
    ffn_w_up = nrm(ks[9], (DEPTH, 2, D_MODEL, D_FF), f32) * D_MODEL ** -0.5
    ffn_w_down = nrm(ks[10], (DEPTH, 2, D_FF, D_MODEL), f32) * D_FF ** -0.5
    w_in = nrm(ks[11], (DEPTH, D_MODEL, IN_WIDTH), f32) * D_MODEL ** -0.5
    conv_w = nrm(ks[12], (DEPTH, CONV_WIDTH, D_CONV), f32) * CONV_WIDTH ** -0.5
    w_out = nrm(ks[13], (DEPTH, D_MODEL, D_MODEL), f32) * D_MODEL ** -0.5
    return {"x_prompt": x_prompt, "x_sample": x_sample, "state_conv": state_conv,
            "cache_kv_w128": cache_kv_w128, "cache_kv_w512": cache_kv_w512,
            "cache_kv_w2048": cache_kv_w2048, "rel_bias": rel_bias, "norm_g": norm_g,
            "ffn_w_gate": ffn_w_gate, "ffn_w_up": ffn_w_up, "ffn_w_down": ffn_w_down,
            "w_in": w_in, "conv_w": conv_w, "w_out": w_out}


def reference(x_prompt, x_sample, state_conv, cache_kv_w128, cache_kv_w512, cache_kv_w2048,
              rel_bias, norm_g, ffn_w_gate, ffn_w_up, ffn_w_down, w_in, conv_w, w_out):
    yp, ys = x_prompt, x_sample
    conv_p, conv_s = [], []
    kv_p = [[] for _ in range(N_GROUPS)]
    kv_s = [[] for _ in range(N_GROUPS)]
    for l in range(DEPTH):
        zero_prev = jnp.zeros((yp.shape[0], CONV_WIDTH - 1, D_CONV), yp.dtype)
        yp, cp, kvp = trunk_layer(yp, norm_g[l], ffn_w_gate[l], ffn_w_up[l], ffn_w_down[l],
                                  w_in[l], conv_w[l], w_out[l], rel_bias, zero_prev, None)
        ys, cs, kvs = trunk_layer(ys, norm_g[l], ffn_w_gate[l], ffn_w_up[l], ffn_w_down[l],
                                  w_in[l], conv_w[l], w_out[l], rel_bias, state_conv[l],
                                  (cache_kv_w128[l], cache_kv_w512[l], cache_kv_w2048[l]))
        conv_p.append(cp)
        conv_s.append(cs)
        for g in range(N_GROUPS):
            kv_p[g].append(kvp[g])
            kv_s[g].append(kvs[g])
    conv_prompt = jnp.stack(conv_p)
    conv_sample = jnp.stack(conv_s)
    kv128_prompt, kv512_prompt, kv2048_prompt = [jnp.stack(t) for t in kv_p]
    kv128_sample, kv512_sample, kv2048_sample = [jnp.stack(t) for t in kv_s]
    return (yp, ys, conv_prompt, conv_sample, kv128_prompt, kv128_sample,
            kv512_prompt, kv512_sample, kv2048_prompt, kv2048_sample)
```

```python
import functools
import math

import jax
import jax.numpy as jnp
import numpy as np
from jax import lax
from jax.experimental import pallas as pl
from jax.experimental.pallas import tpu as pltpu

D_MODEL = 4096
BATCH = 4
SEQ = 2048
DEPTH = 2
DEC_BATCH = 8
DEC_SEQ = 4

D_CONV = D_MODEL
CONV_WIDTH = 3
HEAD_DIM = 128
N_HEADS = D_MODEL // HEAD_DIM
ATTN_WIDTH = N_HEADS * HEAD_DIM
GROUPS = ((128, 1), (512, 4), (2048, 16))
N_GROUPS = len(GROUPS)
N_REL = 128
BAND = 128
N_BUCKETS = 32
MAX_DISTANCE = 2048
D_FF = 11008
RMS_EPS = 1e-6
NEG = -1e30
ATTN_SCALE = 1.0 / math.sqrt(HEAD_DIM)
GATE_BASE = 3 * D_CONV + 3 * N_GROUPS * ATTN_WIDTH
IN_WIDTH = GATE_BASE + 2 * D_MODEL

LANES = 128
SAMPLE_PAD = 8
M_PROMPT = BATCH * SEQ
M_SAMPLE = DEC_BATCH * SAMPLE_PAD
MIB = 1024 * 1024

F32 = jnp.float32
BF16 = jnp.bfloat16


def _params(vmem_mib, n_axes):
    return pltpu.CompilerParams(
        dimension_semantics=("arbitrary",) * n_axes,
        vmem_limit_bytes=int(vmem_mib * MIB))


def _rms(x):
    return x * lax.rsqrt(jnp.mean(x * x, axis=-1, keepdims=True) + RMS_EPS)


def _rms_cast_kernel(x_ref, g_ref, h_ref):
    h_ref[...] = (_rms(x_ref[...]) * g_ref[...]).astype(h_ref.dtype)


def _rms_cast(x, g, out_dtype, tr):
    rows = x.shape[0]
    return pl.pallas_call(
        _rms_cast_kernel,
        grid=(rows // tr,),
        in_specs=[pl.BlockSpec((tr, D_MODEL), lambda i: (i, 0)),
                  pl.BlockSpec((1, D_MODEL), lambda i: (0, 0))],
        out_specs=pl.BlockSpec((tr, D_MODEL), lambda i: (i, 0)),
        out_shape=jax.ShapeDtypeStruct((rows, D_MODEL), out_dtype),
        compiler_params=_params(32, 1),
        name="rms_cast",
    )(x, g.reshape(1, D_MODEL))


def _resid_norm_kernel(x_ref, f_ref, gp_ref, gn_ref, xo_ref, h_ref, *, scale):
    x_new = x_ref[...] + scale * (_rms(f_ref[...]) * gp_ref[...])
    xo_ref[...] = x_new
    h_ref[...] = (_rms(x_new) * gn_ref[...]).astype(h_ref.dtype)


def _resid_norm(x, f, g_post, g_next, scale, h_dtype, tr):
    rows = x.shape[0]
    row_spec = pl.BlockSpec((tr, D_MODEL), lambda i: (i, 0))
    g_spec = pl.BlockSpec((1, D_MODEL), lambda i: (0, 0))
    return pl.pallas_call(
        functools.partial(_resid_norm_kernel, scale=scale),
        grid=(rows // tr,),
        in_specs=[row_spec, row_spec, g_spec, g_spec],
        out_specs=[row_spec, row_spec],
        out_shape=[jax.ShapeDtypeStruct((rows, D_MODEL), F32),
                   jax.ShapeDtypeStruct((rows, D_MODEL), h_dtype)],
        compiler_params=_params(40, 1),
        name="resid_norm",
    )(x, f, g_post.reshape(1, D_MODEL), g_next.reshape(1, D_MODEL))


def _ws_matmul_kernel(*refs, n_w, nt, ck, swiglu):
    a_ref, as_ref = refs[0], refs[1]
    w_refs = refs[2:2 + n_w]
    o_ref, os_ref = refs[2 + n_w], refs[3 + n_w]
    wbufs = refs[4 + n_w:]
    n = pl.program_id(0)
    m = pl.program_id(1)

    @pl.when(n < nt)
    def _cast():
        row = pl.multiple_of(m * ck, 16)
        for w_ref, wb in zip(w_refs, wbufs):
            wb[n % 2, pl.ds(row, ck), :] = w_ref[...].astype(BF16)

    @pl.when(n > 0)
    def _compute():
        slot = (n + 1) % 2

        def compute(x):
            ys = [jnp.dot(x, wb[slot], preferred_element_type=F32) for wb in wbufs]
            if swiglu:
                gate, up = ys
                return gate * jax.nn.sigmoid(gate) * up
            return ys[0]

        o_ref[...] = compute(a_ref[...]).astype(o_ref.dtype)

        @pl.when(m == 0)
        def _sample():
            os_ref[...] = compute(as_ref[...].astype(BF16)).astype(os_ref.dtype)


def _ws_matmul(a, a_s, weights, *, k, n_out, col_off, bm, tn, out_dtype, swiglu, vmem_mib, name):
    m_rows = a.shape[0]
    mt = m_rows // bm
    nt = n_out // tn
    ck = k // mt
    assert mt * bm == m_rows and nt * tn == n_out and ck * mt == k and ck % 16 == 0
    coff = col_off // tn
    assert coff * tn == col_off
    ms = a_s.shape[0]

    def a_map(n, m):
        return (jnp.where(n == 0, 0, m), 0)

    def w_map(prefix):
        def f(n, m):
            return prefix + (jnp.where(n == nt, mt - 1, m), jnp.minimum(n, nt - 1) + coff)
        return f

    def o_map(n, m):
        return (jnp.where(n == 0, 0, m), jnp.maximum(n - 1, 0))

    def os_map(n, m):
        return (0, jnp.maximum(n - 1, 0))

    in_specs = [pl.BlockSpec((bm, k), a_map),
                pl.BlockSpec((ms, k), lambda n, m: (0, 0))]
    for arr, prefix in weights:
        in_specs.append(pl.BlockSpec((None,) * len(prefix) + (ck, tn), w_map(tuple(prefix))))
    return pl.pallas_call(
        functools.partial(_ws_matmul_kernel, n_w=len(weights), nt=nt, ck=ck, swiglu=swiglu),
        grid=(nt + 1, mt),
        in_specs=in_specs,
        out_specs=[pl.BlockSpec((bm, tn), o_map), pl.BlockSpec((ms, tn), os_map)],
        out_shape=[jax.ShapeDtypeStruct((m_rows, n_out), out_dtype),
                   jax.ShapeDtypeStruct((ms, n_out), F32)],
        scratch_shapes=[pltpu.VMEM((2, k, tn), BF16) for _ in weights],
        compiler_params=_params(vmem_mib, 2),
        name=name,
    )(a, a_s, *[arr for arr, _ in weights])


def _t5_bucket(dist):
    dist = np.asarray(dist)
    max_exact = N_BUCKETS // 2
    large = max_exact + (np.log(np.maximum(dist, max_exact) / max_exact)
                         / np.log(MAX_DISTANCE / max_exact)
                         * (N_BUCKETS - max_exact)).astype(np.int32)
    large = np.minimum(large, N_BUCKETS - 1)
    return np.where(dist < max_exact, dist, large).astype(np.int32)


def _group_bias(rel_bias, g, dil):
    buckets = _t5_bucket(np.arange(N_REL + 1) * dil)
    return rel_bias[buckets][:, g * N_HEADS:(g + 1) * N_HEADS].T.astype(F32)


def _band_bias(bias_g):
    a = np.arange(BAND)[:, None]
    c = np.arange(2 * BAND)[None, :]
    delta = a + BAND - c
    in_window = (delta >= 0) & (delta <= N_REL)
    return jnp.where(in_window[None], bias_g[:, np.clip(delta, 0, N_REL)], NEG)


def _sample_bias(bias_g, window, dil):
    t = np.arange(SAMPLE_PAD)[:, None]
    c = np.arange(window)[None, :]
    delta = window + t - c
    valid = (delta % dil == 0) & (delta // dil <= N_REL)
    bc = jnp.where(valid[None], bias_g[:, np.clip(delta // dil, 0, N_REL)], NEG)
    bc = jnp.where((t >= DEC_SEQ)[None], 0.0, bc)
    cn = np.arange(DEC_SEQ)[:, None]
    tn = np.arange(SAMPLE_PAD)[None, :]
    dn = tn - cn
    validn = (dn >= 0) & (dn % dil == 0)
    bn = jnp.where(validn[None], bias_g[:, np.clip(dn // dil, 0, N_REL)], NEG)
    bn = jnp.where((tn >= DEC_SEQ)[None], 0.0, bn)
    bn = jnp.broadcast_to(bn[..., None], (N_HEADS, DEC_SEQ, SAMPLE_PAD, LANES))
    return bc, bn


def _attn_prompt_kernel(q_ref, kc_ref, kp_ref, vc_ref, vp_ref, b_ref, o_ref, lse_ref, *, hb):
    band = pl.program_id(2)
    hblk = pl.program_id(3)
    col = lax.broadcasted_iota(jnp.int32, (BAND, 2 * BAND), 1)
    no_prev = jnp.logical_and(band == 0, col < BAND)
    lane = lax.broadcasted_iota(jnp.int32, (BAND, LANES), 1)

    @pl.when(hblk == 0)
    def _init():
        lse_ref[...] = jnp.zeros_like(lse_ref)

    lse_acc = lse_ref[...]
    for h in range(hb):
        sl = slice(h * HEAD_DIM, (h + 1) * HEAD_DIM)
        q = q_ref[:, sl]
        k2 = jnp.concatenate([kp_ref[:, sl], kc_ref[:, sl]], axis=0)
        v2 = jnp.concatenate([vp_ref[:, sl], vc_ref[:, sl]], axis=0)
        s = lax.dot_general(q, k2, (((1,), (1,)), ((), ())), preferred_element_type=F32)
        s = s * ATTN_SCALE + b_ref[h]
        s = jnp.where(no_prev, NEG, s)
        mx = jnp.max(s, axis=-1, keepdims=True)
        p = jnp.exp(s - mx)
        den = jnp.sum(p, axis=-1, keepdims=True)
        o = jnp.dot((p / den).astype(BF16), v2, preferred_element_type=F32)
        o_ref[:, sl] = o.astype(o_ref.dtype)
        lse_acc = jnp.where(lane == hblk * hb + h, mx + jnp.log(den), lse_acc)
    lse_ref[...] = lse_acc


def _attn_prompt(u, bias_band, g, hb):
    _, dil = GROUPS[g]
    n = SEQ // dil
    nb = n // BAND
    base = 3 * D_CONV + 3 * g * ATTN_WIDTH
    cw = hb * HEAD_DIM
    ud = u.reshape(BATCH * n, dil * IN_WIDTH)

    def in_map(off, prev):
        def f(b, r, i, h):
            row = jnp.maximum(i - 1, 0) if prev else i
            return (b * nb + row, r * (IN_WIDTH // cw) + (base + off) // cw + h)
        return f

    blk = (BAND, cw)
    o, lse = pl.pallas_call(
        functools.partial(_attn_prompt_kernel, hb=hb),
        grid=(BATCH, dil, nb, N_HEADS // hb),
        in_specs=[pl.BlockSpec(blk, in_map(0, False)),
                  pl.BlockSpec(blk, in_map(ATTN_WIDTH, False)),
                  pl.BlockSpec(blk, in_map(ATTN_WIDTH, True)),
                  pl.BlockSpec(blk, in_map(2 * ATTN_WIDTH, False)),
                  pl.BlockSpec(blk, in_map(2 * ATTN_WIDTH, True)),
                  pl.BlockSpec((hb, BAND, 2 * BAND), lambda b, r, i, h: (h, 0, 0))],
        out_specs=[pl.BlockSpec(blk, lambda b, r, i, h: (b * nb + i, r * (ATTN_WIDTH // cw) + h)),
                   pl.BlockSpec((BAND, LANES), lambda b, r, i, h: (b * nb + i, r))],
        out_shape=[jax.ShapeDtypeStruct((BATCH * n, dil * ATTN_WIDTH), BF16),
                   jax.ShapeDtypeStruct((BATCH * n, dil * LANES), F32)],
        compiler_params=_params(40, 4),
        name=f"attn_prompt_g{g}",
    )(ud, ud, ud, ud, ud, bias_band)
    return o.reshape(M_PROMPT, ATTN_WIDTH), lse.reshape(M_PROMPT, LANES)


def _attn_sample_kernel(q_ref, kn_ref, vn_ref, kc_ref, vc_ref, bc_ref, bn_ref, o_ref, lse_ref, *, hb):
    hblk = pl.program_id(1)
    lane = lax.broadcasted_iota(jnp.int32, (SAMPLE_PAD, LANES), 1)

    @pl.when(hblk == 0)
    def _init():
        lse_ref[...] = jnp.zeros_like(lse_ref)

    lse_acc = lse_ref[...]
    for h in range(hb):
        sl = slice(h * HEAD_DIM, (h + 1) * HEAD_DIM)
        q = q_ref[:, sl]
        s_c = lax.dot_general(q, kc_ref[:, sl], (((1,), (1,)), ((), ())),
                              preferred_element_type=F32) * ATTN_SCALE + bc_ref[h]
        s_n = []
        for c in range(DEC_SEQ):
            qk = jnp.sum(q * kn_ref[c:c + 1, sl], axis=-1, keepdims=True)
            s_n.append(qk * ATTN_SCALE + bn_ref[h, c][:, 0:1])
        mx = jnp.max(s_c, axis=-1, keepdims=True)
        for c in range(DEC_SEQ):
            mx = jnp.maximum(mx, s_n[c])
        p_c = jnp.exp(s_c - mx)
        p_n = [jnp.exp(s_n[c] - mx) for c in range(DEC_SEQ)]
        den = jnp.sum(p_c, axis=-1, keepdims=True)
        for c in range(DEC_SEQ):
            den = den + p_n[c]
        o = jnp.dot(p_c / den, vc_ref[:, sl], preferred_element_type=F32)
        for c in range(DEC_SEQ):
            o = o + (p_n[c] / den) * vn_ref[c:c + 1, sl]
        o_ref[:, sl] = o
        lse_acc = jnp.where(lane == hblk * hb + h, mx + jnp.log(den), lse_acc)
    lse_ref[...] = lse_acc


def _attn_sample(u_s, cache, layer, bias_c, bias_n, g, hb):
    window, _ = GROUPS[g]
    base = 3 * D_CONV + 3 * g * ATTN_WIDTH
    cw = hb * HEAD_DIM
    cache2 = cache.reshape(DEPTH, DEC_BATCH, window, 2 * ATTN_WIDTH)

    def u_map(off):
        return lambda b, h: (b, (base + off) // cw + h)

    def c_map(off):
        return lambda b, h: (layer, b, 0, off // cw + h)

    return pl.pallas_call(
        functools.partial(_attn_sample_kernel, hb=hb),
        grid=(DEC_BATCH, N_HEADS // hb),
        in_specs=[pl.BlockSpec((SAMPLE_PAD, cw), u_map(0)),
                  pl.BlockSpec((SAMPLE_PAD, cw), u_map(ATTN_WIDTH)),
                  pl.BlockSpec((SAMPLE_PAD, cw), u_map(2 * ATTN_WIDTH)),
                  pl.BlockSpec((None, None, window, cw), c_map(0)),
                  pl.BlockSpec((None, None, window, cw), c_map(ATTN_WIDTH)),
                  pl.BlockSpec((hb, SAMPLE_PAD, window), lambda b, h: (h, 0, 0)),
                  pl.BlockSpec((hb, DEC_SEQ, SAMPLE_PAD, LANES), lambda b, h: (h, 0, 0, 0))],
        out_specs=[pl.BlockSpec((SAMPLE_PAD, cw), lambda b, h: (b, h)),
                   pl.BlockSpec((SAMPLE_PAD, LANES), lambda b, h: (b, 0))],
        out_shape=[jax.ShapeDtypeStruct((M_SAMPLE, ATTN_WIDTH), F32),
                   jax.ShapeDtypeStruct((M_SAMPLE, LANES), F32)],
        compiler_params=_params(40, 2),
        name=f"attn_sample_g{g}",
    )(u_s, u_s, u_s, cache2, cache2, bias_c, bias_n)


def _mix_kernel(*refs, tr, sample):
    (bg_ref, cg_ref, xc_ref, gc_ref, ga_ref, o0_ref, o1_ref, o2_ref,
     l0_ref, l1_ref, l2_ref, cw_ref) = refs[:12]
    if sample:
        prev_ref, merged_ref, cs_ref = refs[12:]
        n_valid = DEC_SEQ
    else:
        merged_ref, cs_ref, carry_ref = refs[12:]
        prev_ref = carry_ref
        n_valid = tr

        @pl.when(pl.program_id(1) == 0)
        def _zero():
            carry_ref[...] = jnp.zeros_like(carry_ref)

    l0, l1, l2 = l0_ref[...], l1_ref[...], l2_ref[...]
    mx = jnp.maximum(jnp.maximum(l0, l1), l2)
    e0, e1, e2 = jnp.exp(l0 - mx), jnp.exp(l1 - mx), jnp.exp(l2 - mx)
    den = e0 + e1 + e2
    a0, a1, a2 = e0 / den, e1 / den, e2 / den
    row = lax.broadcasted_iota(jnp.int32, (tr, HEAD_DIM), 0)

    for h in range(N_HEADS):
        sl = slice(h * HEAD_DIM, (h + 1) * HEAD_DIM)
        cx = cg_ref[:, sl].astype(F32) * xc_ref[:, sl].astype(F32)
        c0 = prev_ref[0:1, sl]
        c1 = prev_ref[1:2, sl]
        cx1 = jnp.where(row == 0, c1, pltpu.roll(cx, 1, 0))
        cx2 = jnp.where(row == 0, c0, jnp.where(row == 1, c1, pltpu.roll(cx, 2, 0)))
        y = cw_ref[0:1, sl] * cx2
        y = y + cw_ref[1:2, sl] * cx1
        y = y + cw_ref[2:3, sl] * cx
        y_conv = bg_ref[:, sl].astype(F32) * y
        y_attn = (o0_ref[:, sl].astype(F32) * a0[:, h:h + 1]
                  + o1_ref[:, sl].astype(F32) * a1[:, h:h + 1]
                  + o2_ref[:, sl].astype(F32) * a2[:, h:h + 1])
        merged = (jax.nn.sigmoid(gc_ref[:, sl].astype(F32)) * y_conv
                  + jax.nn.sigmoid(ga_ref[:, sl].astype(F32)) * y_attn)
        merged_ref[:, sl] = merged.astype(merged_ref.dtype)
        state = cx[n_valid - 2:n_valid, :]
        cs_ref[:, sl] = state
        if not sample:
            carry_ref[0:2, sl] = state


def _mix_prompt(u, outs, lses, conv_w, layer, tr):
    nrt = SEQ // tr
    cblk = D_MODEL

    def u_spec(col):
        return pl.BlockSpec((tr, cblk), lambda b, r: (b * nrt + r, col // cblk))

    row_spec = pl.BlockSpec((tr, cblk), lambda b, r: (b * nrt + r, 0))
    lse_spec = pl.BlockSpec((tr, LANES), lambda b, r: (b * nrt + r, 0))
    return pl.pallas_call(
        functools.partial(_mix_kernel, tr=tr, sample=False),
        grid=(BATCH, nrt),
        in_specs=[u_spec(0), u_spec(D_CONV), u_spec(2 * D_CONV), u_spec(GATE_BASE),
                  u_spec(GATE_BASE + D_MODEL), row_spec, row_spec, row_spec,
                  lse_spec, lse_spec, lse_spec,
                  pl.BlockSpec((None, CONV_WIDTH, D_CONV), lambda b, r: (layer, 0, 0))],
        out_specs=[row_spec,
                   pl.BlockSpec((None, CONV_WIDTH - 1, D_CONV), lambda b, r: (b, 0, 0))],
        out_shape=[jax.ShapeDtypeStruct((M_PROMPT, D_MODEL), BF16),
                   jax.ShapeDtypeStruct((BATCH, CONV_WIDTH - 1, D_CONV), F32)],
        scratch_shapes=[pltpu.VMEM((8, D_CONV), F32)],
        compiler_params=_params(40, 2),
        name="mix_prompt",
    )(u, u, u, u, u, *outs, *lses, conv_w)


def _mix_sample(u_s, outs, lses, conv_w, state_conv, layer):
    tr = SAMPLE_PAD
    cblk = D_MODEL

    def u_spec(col):
        return pl.BlockSpec((tr, cblk), lambda b: (b, col // cblk))

    row_spec = pl.BlockSpec((tr, cblk), lambda b: (b, 0))
    lse_spec = pl.BlockSpec((tr, LANES), lambda b: (b, 0))
    state_spec = pl.BlockSpec((None, None, CONV_WIDTH - 1, D_CONV), lambda b: (layer, b, 0, 0))
    return pl.pallas_call(
        functools.partial(_mix_kernel, tr=tr, sample=True),
        grid=(DEC_BATCH,),
        in_specs=[u_spec(0), u_spec(D_CONV), u_spec(2 * D_CONV), u_spec(GATE_BASE),
                  u_spec(GATE_BASE + D_MODEL), row_spec, row_spec, row_spec,
                  lse_spec, lse_spec, lse_spec,
                  pl.BlockSpec((None, CONV_WIDTH, D_CONV), lambda b: (layer, 0, 0)),
                  state_spec],
        out_specs=[row_spec,
                   pl.BlockSpec((None, CONV_WIDTH - 1, D_CONV), lambda b: (b, 0, 0))],
        out_shape=[jax.ShapeDtypeStruct((M_SAMPLE, D_MODEL), F32),
                   jax.ShapeDtypeStruct((DEC_BATCH, CONV_WIDTH - 1, D_CONV), F32)],
        compiler_params=_params(40, 1),
        name="mix_sample",
    )(u_s, u_s, u_s, u_s, u_s, *outs, *lses, conv_w, state_conv)


ATTN_HB = 8
SAMPLE_HB = (16, 8, 4)
ROW_TILE = 128


def _ffn(h_p, h_s, w_gate, w_up, w_down, layer, idx):
    act_p, act_s = _ws_matmul(
        h_p, h_s, [(w_gate, (layer, idx)), (w_up, (layer, idx))],
        k=D_MODEL, n_out=D_FF, col_off=0, bm=1024, tn=256, out_dtype=BF16, swiglu=True,
        vmem_mib=48, name="ffn_gate_up")
    return _ws_matmul(
        act_p, act_s, [(w_down, (layer, idx))],
        k=D_FF, n_out=D_MODEL, col_off=0, bm=512, tn=512, out_dtype=F32, swiglu=False,
        vmem_mib=58, name="ffn_down")


def kernel(x_prompt, x_sample, state_conv, cache_kv_w128, cache_kv_w512, cache_kv_w2048, rel_bias,
           norm_g, ffn_w_gate, ffn_w_up, ffn_w_down, w_in, conv_w, w_out):
    caches = (cache_kv_w128, cache_kv_w512, cache_kv_w2048)
    xp = x_prompt.reshape(M_PROMPT, D_MODEL)
    xs = jnp.pad(x_sample, ((0, 0), (0, SAMPLE_PAD - DEC_SEQ), (0, 0))).reshape(M_SAMPLE, D_MODEL)

    band_bias, samp_bias = [], []
    for g, (window, dil) in enumerate(GROUPS):
        bias_g = _group_bias(rel_bias, g, dil)
        band_bias.append(_band_bias(bias_g))
        samp_bias.append(_sample_bias(bias_g, window, dil))

    h_p = _rms_cast(xp, norm_g[0, 0], BF16, ROW_TILE)
    h_s = _rms_cast(xs, norm_g[0, 0], F32, M_SAMPLE)

    conv_p, conv_s = [], []
    kv_p = [[] for _ in range(N_GROUPS)]
    kv_s = [[] for _ in range(N_GROUPS)]
    for l in range(DEPTH):
        f_p, f_s = _ffn(h_p, h_s, ffn_w_gate, ffn_w_up, ffn_w_down, l, 0)
        xp, h_p = _resid_norm(xp, f_p, norm_g[l, 1], norm_g[l, 2], 0.5, BF16, ROW_TILE)
        xs, h_s = _resid_norm(xs, f_s, norm_g[l, 1], norm_g[l, 2], 0.5, F32, M_SAMPLE)

        u_p, u_s = _ws_matmul(
            h_p, h_s, [(w_in, (l,))], k=D_MODEL, n_out=IN_WIDTH, col_off=0, bm=1024, tn=1024,
            out_dtype=BF16, swiglu=False, vmem_mib=56, name="w_in")
        outs_p, lses_p, outs_s, lses_s = [], [], [], []
        for g, (window, dil) in enumerate(GROUPS):
            o, lse = _attn_prompt(u_p, band_bias[g], g, ATTN_HB)
            outs_p.append(o)
            lses_p.append(lse)
            o, lse = _attn_sample(u_s, caches[g], l, samp_bias[g][0], samp_bias[g][1], g, SAMPLE_HB[g])
            outs_s.append(o)
            lses_s.append(lse)
            base = 3 * D_CONV + 3 * g * ATTN_WIDTH
            keep = min(window, SEQ)
            kv = u_p.reshape(BATCH, SEQ, IN_WIDTH)[:, SEQ - keep:, base + ATTN_WIDTH:base + 3 * ATTN_WIDTH]
            kv_p[g].append(kv.astype(F32).reshape(BATCH, keep, 2, N_HEADS, HEAD_DIM))
            kv = u_s.reshape(DEC_BATCH, SAMPLE_PAD, IN_WIDTH)[:, :DEC_SEQ,
                                                               base + ATTN_WIDTH:base + 3 * ATTN_WIDTH]
            kv_s[g].append(kv.reshape(DEC_BATCH, DEC_SEQ, 2, N_HEADS, HEAD_DIM))
        merged_p, cs_p = _mix_prompt(u_p, outs_p, lses_p, conv_w, l, ROW_TILE)
        merged_s, cs_s = _mix_sample(u_s, outs_s, lses_s, conv_w, state_conv, l)
        conv_p.append(cs_p)
        conv_s.append(cs_s)
        m_p, m_s = _ws_matmul(
            merged_p, merged_s, [(w_out, (l,))], k=D_MODEL, n_out=D_MODEL, col_off=0, bm=1024, tn=1024,
            out_dtype=F32, swiglu=False, vmem_mib=56, name="w_out")
        xp, h_p = _resid_norm(xp, m_p, norm_g[l, 3], norm_g[l, 4], 1.0, BF16, ROW_TILE)
        xs, h_s = _resid_norm(xs, m_s, norm_g[l, 3], norm_g[l, 4], 1.0, F32, M_SAMPLE)

        f_p, f_s = _ffn(h_p, h_s, ffn_w_gate, ffn_w_up, ffn_w_down, l, 1)
        g_next = norm_g[l + 1, 0] if l + 1 < DEPTH else norm_g[l, 5]
        xp, h_p = _resid_norm(xp, f_p, norm_g[l, 5], g_next, 0.5, BF16, ROW_TILE)
        xs, h_s = _resid_norm(xs, f_s, norm_g[l, 5], g_next, 0.5, F32, M_SAMPLE)

    y_prompt = xp.reshape(BATCH, SEQ, D_MODEL)
    y_sample = xs.reshape(DEC_BATCH, SAMPLE_PAD, D_MODEL)[:, :DEC_SEQ]
    kv128_p, kv512_p, kv2048_p = [jnp.stack(t) for t in kv_p]
    kv128_s, kv512_s, kv2048_s = [jnp.stack(t) for t in kv_s]
    return (y_prompt, y_sample, jnp.stack(conv_p), jnp.stack(conv_s),
            kv128_p, kv128_s, kv512_p, kv512_s, kv2048_p, kv2048_s)
```

```python
import functools
import math

import jax
import jax.numpy as jnp
import numpy as np
from jax import lax
from jax.experimental import pallas as pl
from jax.experimental.pallas import tpu as pltpu

D_MODEL = 4096
BATCH = 4
SEQ = 2048
DEPTH = 2
DEC_BATCH = 8
DEC_SEQ = 4

D_CONV = D_MODEL
CONV_WIDTH = 3
HEAD_DIM = 128
N_HEADS = D_MODEL // HEAD_DIM
ATTN_WIDTH = N_HEADS * HEAD_DIM
GROUPS = ((128, 1), (512, 4), (2048, 16))
N_GROUPS = len(GROUPS)
N_REL = 128
BAND = 128
N_BUCKETS = 32
MAX_DISTANCE = 2048
D_FF = 11008
RMS_EPS = 1e-6
NEG = -1e30
ATTN_SCALE = 1.0 / math.sqrt(HEAD_DIM)
QKV_BASE = 3 * D_CONV
QKV_WIDTH = 3 * N_GROUPS * ATTN_WIDTH
GATE_BASE = QKV_BASE + QKV_WIDTH
IN_WIDTH = GATE_BASE + 2 * D_MODEL

LANES = 128
SAMPLE_PAD = 8
M_PROMPT = BATCH * SEQ
M_SAMPLE = DEC_BATCH * SAMPLE_PAD
MIB = 1024 * 1024

F32 = jnp.float32
BF16 = jnp.bfloat16


def _params(vmem_mib, n_axes):
    return pltpu.CompilerParams(
        dimension_semantics=("arbitrary",) * n_axes,
        vmem_limit_bytes=int(vmem_mib * MIB))


def _rms(x):
    return x * lax.rsqrt(jnp.mean(x * x, axis=-1, keepdims=True) + RMS_EPS)


def _rms_cast_kernel(x_ref, g_ref, h_ref):
    h_ref[...] = (_rms(x_ref[...]) * g_ref[...]).astype(h_ref.dtype)


def _rms_cast(x, g, out_dtype, tr):
    rows = x.shape[0]
    return pl.pallas_call(
        _rms_cast_kernel,
        grid=(rows // tr,),
        in_specs=[pl.BlockSpec((tr, D_MODEL), lambda i: (i, 0)),
                  pl.BlockSpec((1, D_MODEL), lambda i: (0, 0))],
        out_specs=pl.BlockSpec((tr, D_MODEL), lambda i: (i, 0)),
        out_shape=jax.ShapeDtypeStruct((rows, D_MODEL), out_dtype),
        compiler_params=_params(32, 1),
        name="rms_cast",
    )(x, g.reshape(1, D_MODEL))


def _resid_norm_kernel(x_ref, f_ref, gp_ref, gn_ref, xo_ref, h_ref, *, scale):
    x_new = x_ref[...] + scale * (_rms(f_ref[...]) * gp_ref[...])
    xo_ref[...] = x_new
    h_ref[...] = (_rms(x_new) * gn_ref[...]).astype(h_ref.dtype)


def _resid_norm(x, f, g_post, g_next, scale, h_dtype, tr):
    rows = x.shape[0]
    row_spec = pl.BlockSpec((tr, D_MODEL), lambda i: (i, 0))
    g_spec = pl.BlockSpec((1, D_MODEL), lambda i: (0, 0))
    return pl.pallas_call(
        functools.partial(_resid_norm_kernel, scale=scale),
        grid=(rows // tr,),
        in_specs=[row_spec, row_spec, g_spec, g_spec],
        out_specs=[row_spec, row_spec],
        out_shape=[jax.ShapeDtypeStruct((rows, D_MODEL), F32),
                   jax.ShapeDtypeStruct((rows, D_MODEL), h_dtype)],
        compiler_params=_params(40, 1),
        name="resid_norm",
    )(x, f, g_post.reshape(1, D_MODEL), g_next.reshape(1, D_MODEL))


def _ws_matmul_kernel(*refs, n_w, nt, ck, swiglu, slab_out):
    a_ref, as_ref = refs[0], refs[1]
    w_refs = refs[2:2 + n_w]
    o_ref, os_ref = refs[2 + n_w], refs[3 + n_w]
    wbufs = refs[4 + n_w:]
    n = pl.program_id(0)
    m = pl.program_id(1)

    @pl.when(n < nt)
    def _cast():
        row = pl.multiple_of(m * ck, 16)
        for w_ref, wb in zip(w_refs, wbufs):
            wb[n % 2, pl.ds(row, ck), :] = w_ref[...].astype(BF16)

    @pl.when(n > 0)
    def _compute():
        slot = (n + 1) % 2

        def compute(x):
            ys = [jnp.dot(x, wb[slot], preferred_element_type=F32) for wb in wbufs]
            if swiglu:
                gate, up = ys
                return gate * jax.nn.sigmoid(gate) * up
            return ys[0]

        res = compute(a_ref[...]).astype(o_ref.dtype)
        if slab_out:
            for j in range(o_ref.shape[0]):
                o_ref[j] = res[:, j * LANES:(j + 1) * LANES]
        else:
            o_ref[...] = res

        @pl.when(m == 0)
        def _sample():
            os_ref[...] = compute(as_ref[...].astype(BF16)).astype(os_ref.dtype)


def _ws_matmul(a, a_s, weights, *, k, n_out, col_off, bm, tn, out_dtype, swiglu, vmem_mib, name,
               slab_out=False):
    m_rows = a.shape[0]
    mt = m_rows // bm
    nt = n_out // tn
    ck = k // mt
    assert mt * bm == m_rows and nt * tn == n_out and ck * mt == k and ck % 16 == 0
    coff = col_off // tn
    assert coff * tn == col_off
    ms = a_s.shape[0]

    def a_map(n, m):
        return (jnp.where(n == 0, 0, m), 0)

    def w_map(prefix):
        def f(n, m):
            return prefix + (jnp.where(n == nt, mt - 1, m), jnp.minimum(n, nt - 1) + coff)
        return f

    def os_map(n, m):
        return (0, jnp.maximum(n - 1, 0))

    if slab_out:
        o_spec = pl.BlockSpec((tn // LANES, bm, LANES),
                              lambda n, m: (jnp.maximum(n - 1, 0), jnp.where(n == 0, 0, m), 0))
        o_shape = jax.ShapeDtypeStruct((n_out // LANES, m_rows, LANES), out_dtype)
    else:
        o_spec = pl.BlockSpec((bm, tn),
                              lambda n, m: (jnp.where(n == 0, 0, m), jnp.maximum(n - 1, 0)))
        o_shape = jax.ShapeDtypeStruct((m_rows, n_out), out_dtype)

    in_specs = [pl.BlockSpec((bm, k), a_map),
                pl.BlockSpec((ms, k), lambda n, m: (0, 0))]
    for arr, prefix in weights:
        in_specs.append(pl.BlockSpec((None,) * len(prefix) + (ck, tn), w_map(tuple(prefix))))
    return pl.pallas_call(
        functools.partial(_ws_matmul_kernel, n_w=len(weights), nt=nt, ck=ck, swiglu=swiglu,
                          slab_out=slab_out),
        grid=(nt + 1, mt),
        in_specs=in_specs,
        out_specs=[o_spec, pl.BlockSpec((ms, tn), os_map)],
        out_shape=[o_shape, jax.ShapeDtypeStruct((ms, n_out), F32)],
        scratch_shapes=[pltpu.VMEM((2, k, tn), BF16) for _ in weights],
        compiler_params=_params(vmem_mib, 2),
        name=name,
    )(a, a_s, *[arr for arr, _ in weights])


def _t5_bucket(dist):
    dist = np.asarray(dist)
    max_exact = N_BUCKETS // 2
    large = max_exact + (np.log(np.maximum(dist, max_exact) / max_exact)
                         / np.log(MAX_DISTANCE / max_exact)
                         * (N_BUCKETS - max_exact)).astype(np.int32)
    large = np.minimum(large, N_BUCKETS - 1)
    return np.where(dist < max_exact, dist, large).astype(np.int32)


def _group_bias(rel_bias, g, dil):
    buckets = _t5_bucket(np.arange(N_REL + 1) * dil)
    return rel_bias[buckets][:, g * N_HEADS:(g + 1) * N_HEADS].T.astype(F32)


def _neg(*shape):
    return jnp.full(shape, NEG, F32)


def _band_bias(bias_g):
    period = 3 * BAND
    diag = jnp.concatenate([_neg(N_HEADS, BAND - 1), bias_g[:, ::-1],
                            _neg(N_HEADS, period - 2 * BAND)], axis=1)
    flat = jnp.tile(diag, (1, BAND))[:, :BAND * (period - 1)]
    return flat.reshape(N_HEADS, BAND, period - 1)[:, :, BAND - 1:3 * BAND - 1]


def _sample_bias(bias_g, window, dil):
    pat = jnp.concatenate([bias_g[:, ::-1, None], _neg(N_HEADS, N_REL + 1, dil - 1)], axis=2)
    pat = pat.reshape(N_HEADS, (N_REL + 1) * dil)
    rows = []
    for t in range(SAMPLE_PAD):
        if t < DEC_SEQ:
            row = jnp.concatenate([_neg(N_HEADS, t), pat, _neg(N_HEADS, DEC_SEQ)], axis=1)
            rows.append(row[:, :window + DEC_SEQ])
        else:
            rows.append(jnp.zeros((N_HEADS, window + DEC_SEQ), F32))
    full = jnp.stack(rows, axis=1)
    bc = full[:, :, :window]
    bn = full[:, :, window:].transpose(0, 2, 1)
    bn = jnp.broadcast_to(bn[..., None], (N_HEADS, DEC_SEQ, SAMPLE_PAD, LANES))
    return bc, bn


def _attn_prompt_kernel(q_ref, k_ref, v_ref, b_ref, o_ref, lse_ref, *, hb, dil, unroll):
    hblk = pl.program_id(1)
    nb = SEQ // dil // BAND
    lane = lax.broadcasted_iota(jnp.int32, (BAND, LANES), 1)
    col = lax.broadcasted_iota(jnp.int32, (BAND, 2 * BAND), 1)

    @pl.when(hblk == 0)
    def _init():
        lse_ref[...] = jnp.zeros_like(lse_ref)

    def rows_at(row0):
        if dil == 1:
            return pl.ds(pl.multiple_of(row0, BAND), BAND)
        return pl.ds(row0, BAND, stride=dil)

    def trip(t, carry):
        chains = []
        for u in range(unroll):
            idx = t * unroll + u
            if nb == 1:
                row0 = idx
                mask = None
            else:
                beta = lax.rem(idx, nb)
                row0 = lax.div(idx, nb) + beta * (BAND * dil)
                first = beta == 0
                prev = rows_at(jnp.where(first, row0, row0 - BAND * dil))
                mask = jnp.logical_and(first, col < BAND)
            cur = rows_at(row0)
            for h in range(hb):
                q = q_ref[h, cur, :].astype(BF16)
                k2 = k_ref[h, cur, :].astype(BF16)
                v2 = v_ref[h, cur, :].astype(BF16)
                if nb == 1:
                    bias = b_ref[h][:, BAND:]
                else:
                    bias = b_ref[h]
                    k2 = jnp.concatenate([k_ref[h, prev, :].astype(BF16), k2], axis=0)
                    v2 = jnp.concatenate([v_ref[h, prev, :].astype(BF16), v2], axis=0)
                s = lax.dot_general(q, k2, (((1,), (1,)), ((), ())), preferred_element_type=F32)
                chains.append(dict(u=u, h=h, cur=cur, s=s, bias=bias, mask=mask, v2=v2))
        for c in chains:
            s = c["s"] * ATTN_SCALE + c["bias"]
            if c["mask"] is not None:
                s = jnp.where(c["mask"], NEG, s)
            mx = jnp.max(s, axis=-1, keepdims=True)
            p = jnp.exp(s - mx)
            den = jnp.sum(p, axis=-1, keepdims=True)
            c["p"] = (p / den).astype(BF16)
            c["lse"] = mx + jnp.log(den)
        for c in chains:
            c["o"] = jnp.dot(c["p"], c["v2"], preferred_element_type=F32)
        for u in range(unroll):
            mine = [c for c in chains if c["u"] == u]
            cur = mine[0]["cur"]
            lse_acc = lse_ref[cur, :]
            for c in mine:
                o_ref[c["h"], cur, :] = c["o"]
                lse_acc = jnp.where(lane == hblk * hb + c["h"], c["lse"], lse_acc)
            lse_ref[cur, :] = lse_acc
        return carry

    lax.fori_loop(0, dil * nb // unroll, trip, 0)


def _attn_prompt(qkv, bias_band, g, hb, unroll):
    _, dil = GROUPS[g]

    def slab_map(which):
        first = (3 * g + which) * N_HEADS // hb
        return lambda b, h: (first + h, b, 0)

    blk = (hb, SEQ, LANES)
    return pl.pallas_call(
        functools.partial(_attn_prompt_kernel, hb=hb, dil=dil, unroll=unroll),
        grid=(BATCH, N_HEADS // hb),
        in_specs=[pl.BlockSpec(blk, slab_map(0)),
                  pl.BlockSpec(blk, slab_map(1)),
                  pl.BlockSpec(blk, slab_map(2)),
                  pl.BlockSpec((hb, BAND, 2 * BAND), lambda b, h: (h, 0, 0))],
        out_specs=[pl.BlockSpec(blk, lambda b, h: (h, b, 0)),
                   pl.BlockSpec((SEQ, LANES), lambda b, h: (b, 0))],
        out_shape=[jax.ShapeDtypeStruct((N_HEADS, M_PROMPT, LANES), F32),
                   jax.ShapeDtypeStruct((M_PROMPT, LANES), F32)],
        compiler_params=_params(40, 2),
        name=f"attn_prompt_g{g}",
    )(qkv, qkv, qkv, bias_band)


def _attn_sample_kernel(q_ref, kn_ref, vn_ref, kc_ref, vc_ref, bc_ref, bn_ref, o_ref, lse_ref, *, hb):
    hblk = pl.program_id(1)
    lane = lax.broadcasted_iota(jnp.int32, (SAMPLE_PAD, LANES), 1)

    @pl.when(hblk == 0)
    def _init():
        lse_ref[...] = jnp.zeros_like(lse_ref)

    heads = []
    for h in range(hb):
        sl = slice(h * HEAD_DIM, (h + 1) * HEAD_DIM)
        q = q_ref[:, sl]
        s_c = lax.dot_general(q, kc_ref[:, sl], (((1,), (1,)), ((), ())),
                              preferred_element_type=F32)
        heads.append(dict(h=h, sl=sl, q=q, s_c=s_c))
    for c_ in heads:
        h, sl, q = c_["h"], c_["sl"], c_["q"]
        s_c = c_["s_c"] * ATTN_SCALE + bc_ref[h]
        s_n = []
        for c in range(DEC_SEQ):
            qk = jnp.sum(q * kn_ref[c:c + 1, sl], axis=-1, keepdims=True)
            s_n.append(qk * ATTN_SCALE + bn_ref[h, c][:, 0:1])
        mx = jnp.max(s_c, axis=-1, keepdims=True)
        for c in range(DEC_SEQ):
            mx = jnp.maximum(mx, s_n[c])
        p_c = jnp.exp(s_c - mx)
        p_n = [jnp.exp(s_n[c] - mx) for c in range(DEC_SEQ)]
        den = jnp.sum(p_c, axis=-1, keepdims=True)
        for c in range(DEC_SEQ):
            den = den + p_n[c]
        c_["p_c"] = p_c / den
        c_["p_n"] = [p_n[c] / den for c in range(DEC_SEQ)]
        c_["lse"] = mx + jnp.log(den)
    for c_ in heads:
        c_["o"] = jnp.dot(c_["p_c"], vc_ref[:, c_["sl"]], preferred_element_type=F32)
    lse_acc = lse_ref[...]
    for c_ in heads:
        o = c_["o"]
        for c in range(DEC_SEQ):
            o = o + c_["p_n"][c] * vn_ref[c:c + 1, c_["sl"]]
        o_ref[:, c_["sl"]] = o
        lse_acc = jnp.where(lane == hblk * hb + c_["h"], c_["lse"], lse_acc)
    lse_ref[...] = lse_acc


def _attn_sample(qkv_s, cache, layer, bias_c, bias_n, g, hb):
    window, _ = GROUPS[g]
    base = 3 * g * ATTN_WIDTH
    cw = hb * HEAD_DIM
    cache2 = cache.reshape(DEPTH, DEC_BATCH, window, 2 * ATTN_WIDTH)

    def u_map(off):
        return lambda b, h: (b, (base + off) // cw + h)

    def c_map(off):
        return lambda b, h: (layer, b, 0, off // cw + h)

    return pl.pallas_call(
        functools.partial(_attn_sample_kernel, hb=hb),
        grid=(DEC_BATCH, N_HEADS // hb),
        in_specs=[pl.BlockSpec((SAMPLE_PAD, cw), u_map(0)),
                  pl.BlockSpec((SAMPLE_PAD, cw), u_map(ATTN_WIDTH)),
                  pl.BlockSpec((SAMPLE_PAD, cw), u_map(2 * ATTN_WIDTH)),
                  pl.BlockSpec((None, None, window, cw), c_map(0)),
                  pl.BlockSpec((None, None, window, cw), c_map(ATTN_WIDTH)),
                  pl.BlockSpec((hb, SAMPLE_PAD, window), lambda b, h: (h, 0, 0)),
                  pl.BlockSpec((hb, DEC_SEQ, SAMPLE_PAD, LANES), lambda b, h: (h, 0, 0, 0))],
        out_specs=[pl.BlockSpec((SAMPLE_PAD, cw), lambda b, h: (b, h)),
                   pl.BlockSpec((SAMPLE_PAD, LANES), lambda b, h: (b, 0))],
        out_shape=[jax.ShapeDtypeStruct((M_SAMPLE, ATTN_WIDTH), F32),
                   jax.ShapeDtypeStruct((M_SAMPLE, LANES), F32)],
        compiler_params=_params(40, 2),
        name=f"attn_sample_g{g}",
    )(qkv_s, qkv_s, qkv_s, cache2, cache2, bias_c, bias_n)


def _mix_kernel(*refs, tr, sample):
    (bg_ref, cg_ref, xc_ref, gc_ref, ga_ref, o0_ref, o1_ref, o2_ref,
     l0_ref, l1_ref, l2_ref, cw_ref) = refs[:12]
    if sample:
        prev_ref, merged_ref, cs_ref = refs[12:]
        n_valid = DEC_SEQ
    else:
        merged_ref, cs_ref, carry_ref = refs[12:]
        prev_ref = carry_ref
        n_valid = tr

        @pl.when(pl.program_id(1) == 0)
        def _zero():
            carry_ref[...] = jnp.zeros_like(carry_ref)

    l0, l1, l2 = l0_ref[...], l1_ref[...], l2_ref[...]
    mx = jnp.maximum(jnp.maximum(l0, l1), l2)
    e0, e1, e2 = jnp.exp(l0 - mx), jnp.exp(l1 - mx), jnp.exp(l2 - mx)
    den = e0 + e1 + e2
    a0, a1, a2 = e0 / den, e1 / den, e2 / den
    row = lax.broadcasted_iota(jnp.int32, (tr, HEAD_DIM), 0)

    for h in range(N_HEADS):
        sl = slice(h * HEAD_DIM, (h + 1) * HEAD_DIM)

        def head_out(o_ref):
            return o_ref[:, sl] if sample else o_ref[h]

        cx = cg_ref[:, sl].astype(F32) * xc_ref[:, sl].astype(F32)
        c0 = prev_ref[0:1, sl]
        c1 = prev_ref[1:2, sl]
        cx1 = jnp.where(row == 0, c1, pltpu.roll(cx, 1, 0))
        cx2 = jnp.where(row == 0, c0, jnp.where(row == 1, c1, pltpu.roll(cx, 2, 0)))
        y = cw_ref[0:1, sl] * cx2
        y = y + cw_ref[1:2, sl] * cx1
        y = y + cw_ref[2:3, sl] * cx
        y_conv = bg_ref[:, sl].astype(F32) * y
        y_attn = (head_out(o0_ref) * a0[:, h:h + 1]
                  + head_out(o1_ref) * a1[:, h:h + 1]
                  + head_out(o2_ref) * a2[:, h:h + 1])
        merged = (jax.nn.sigmoid(gc_ref[:, sl].astype(F32)) * y_conv
                  + jax.nn.sigmoid(ga_ref[:, sl].astype(F32)) * y_attn)
        merged_ref[:, sl] = merged.astype(merged_ref.dtype)
        state = cx[n_valid - 2:n_valid, :]
        cs_ref[:, sl] = state
        if not sample:
            carry_ref[0:2, sl] = state


def _mix_prompt(u_conv, u_gate, outs, lses, conv_w, layer, tr):
    nrt = SEQ // tr

    def col_spec(j):
        return pl.BlockSpec((tr, D_MODEL), lambda b, r: (b * nrt + r, j))

    slab_spec = pl.BlockSpec((N_HEADS, tr, LANES), lambda b, r: (0, b * nrt + r, 0))
    lse_spec = pl.BlockSpec((tr, LANES), lambda b, r: (b * nrt + r, 0))
    return pl.pallas_call(
        functools.partial(_mix_kernel, tr=tr, sample=False),
        grid=(BATCH, nrt),
        in_specs=[col_spec(0), col_spec(1), col_spec(2), col_spec(0), col_spec(1),
                  slab_spec, slab_spec, slab_spec, lse_spec, lse_spec, lse_spec,
                  pl.BlockSpec((None, CONV_WIDTH, D_CONV), lambda b, r: (layer, 0, 0))],
        out_specs=[col_spec(0),
                   pl.BlockSpec((None, CONV_WIDTH - 1, D_CONV), lambda b, r: (b, 0, 0))],
        out_shape=[jax.ShapeDtypeStruct((M_PROMPT, D_MODEL), BF16),
                   jax.ShapeDtypeStruct((BATCH, CONV_WIDTH - 1, D_CONV), F32)],
        scratch_shapes=[pltpu.VMEM((8, D_CONV), F32)],
        compiler_params=_params(48, 2),
        name="mix_prompt",
    )(u_conv, u_conv, u_conv, u_gate, u_gate, *outs, *lses, conv_w)


def _mix_sample(u_conv, u_gate, outs, lses, conv_w, state_conv, layer):
    tr = SAMPLE_PAD

    def col_spec(j):
        return pl.BlockSpec((tr, D_MODEL), lambda b: (b, j))

    lse_spec = pl.BlockSpec((tr, LANES), lambda b: (b, 0))
    state_spec = pl.BlockSpec((None, None, CONV_WIDTH - 1, D_CONV), lambda b: (layer, b, 0, 0))
    return pl.pallas_call(
        functools.partial(_mix_kernel, tr=tr, sample=True),
        grid=(DEC_BATCH,),
        in_specs=[col_spec(0), col_spec(1), col_spec(2), col_spec(0), col_spec(1),
                  col_spec(0), col_spec(0), col_spec(0), lse_spec, lse_spec, lse_spec,
                  pl.BlockSpec((None, CONV_WIDTH, D_CONV), lambda b: (layer, 0, 0)),
                  state_spec],
        out_specs=[col_spec(0),
                   pl.BlockSpec((None, CONV_WIDTH - 1, D_CONV), lambda b: (b, 0, 0))],
        out_shape=[jax.ShapeDtypeStruct((M_SAMPLE, D_MODEL), F32),
                   jax.ShapeDtypeStruct((DEC_BATCH, CONV_WIDTH - 1, D_CONV), F32)],
        compiler_params=_params(40, 1),
        name="mix_sample",
    )(u_conv, u_conv, u_conv, u_gate, u_gate, *outs, *lses, conv_w, state_conv)


ATTN_HB = 2
ATTN_UNROLL = (4, 4, 4)
SAMPLE_HB = (16, 8, 4)
ROW_TILE = 128


def _ffn(h_p, h_s, w_gate, w_up, w_down, layer, idx):
    act_p, act_s = _ws_matmul(
        h_p, h_s, [(w_gate, (layer, idx)), (w_up, (layer, idx))],
        k=D_MODEL, n_out=D_FF, col_off=0, bm=2048, tn=256, out_dtype=BF16, swiglu=True,
        vmem_mib=56, name="ffn_gate_up")
    return _ws_matmul(
        act_p, act_s, [(w_down, (layer, idx))],
        k=D_FF, n_out=D_MODEL, col_off=0, bm=512, tn=512, out_dtype=F32, swiglu=False,
        vmem_mib=58, name="ffn_down")


def _in_proj(h_p, h_s, w_in, layer, col_off, n_out, out_dtype, slab_out, name):
    return _ws_matmul(
        h_p, h_s, [(w_in, (layer,))], k=D_MODEL, n_out=n_out, col_off=col_off, bm=1024, tn=1024,
        out_dtype=out_dtype, swiglu=False, vmem_mib=56, name=name, slab_out=slab_out)


def kernel(x_prompt, x_sample, state_conv, cache_kv_w128, cache_kv_w512, cache_kv_w2048, rel_bias,
           norm_g, ffn_w_gate, ffn_w_up, ffn_w_down, w_in, conv_w, w_out):
    caches = (cache_kv_w128, cache_kv_w512, cache_kv_w2048)
    xp = x_prompt.reshape(M_PROMPT, D_MODEL)
    xs = jnp.pad(x_sample, ((0, 0), (0, SAMPLE_PAD - DEC_SEQ), (0, 0))).reshape(M_SAMPLE, D_MODEL)

    band_bias, samp_bias = [], []
    for g, (window, dil) in enumerate(GROUPS):
        bias_g = _group_bias(rel_bias, g, dil)
        band_bias.append(_band_bias(bias_g))
        samp_bias.append(_sample_bias(bias_g, window, dil))

    h_p = _rms_cast(xp, norm_g[0, 0], BF16, ROW_TILE)
    h_s = _rms_cast(xs, norm_g[0, 0], F32, M_SAMPLE)

    conv_p, conv_s = [], []
    kv_p = [[] for _ in range(N_GROUPS)]
    kv_s = [[] for _ in range(N_GROUPS)]
    for l in range(DEPTH):
        f_p, f_s = _ffn(h_p, h_s, ffn_w_gate, ffn_w_up, ffn_w_down, l, 0)
        xp, h_p = _resid_norm(xp, f_p, norm_g[l, 1], norm_g[l, 2], 0.5, BF16, ROW_TILE)
        xs, h_s = _resid_norm(xs, f_s, norm_g[l, 1], norm_g[l, 2], 0.5, F32, M_SAMPLE)

        conv_in_p, conv_in_s = _in_proj(h_p, h_s, w_in, l, 0, QKV_BASE, BF16, False, "w_in_conv")
        qkv_p, qkv_s = _in_proj(h_p, h_s, w_in, l, QKV_BASE, QKV_WIDTH, F32, True, "w_in_qkv")
        gate_p, gate_s = _in_proj(h_p, h_s, w_in, l, GATE_BASE, 2 * D_MODEL, BF16, False, "w_in_gate")

        outs_p, lses_p, outs_s, lses_s = [], [], [], []
        for g, (window, dil) in enumerate(GROUPS):
            o, lse = _attn_prompt(qkv_p, band_bias[g], g, ATTN_HB, ATTN_UNROLL[g])
            outs_p.append(o)
            lses_p.append(lse)
            o, lse = _attn_sample(qkv_s, caches[g], l, samp_bias[g][0], samp_bias[g][1], g, SAMPLE_HB[g])
            outs_s.append(o)
            lses_s.append(lse)
            keep = min(window, SEQ)
            first = (3 * g + 1) * N_HEADS
            kv = qkv_p[first:first + 2 * N_HEADS].reshape(2, N_HEADS, BATCH, SEQ, HEAD_DIM)
            kv_p[g].append(kv[:, :, :, SEQ - keep:].transpose(2, 3, 0, 1, 4))
            base = 3 * g * ATTN_WIDTH
            kv = qkv_s.reshape(DEC_BATCH, SAMPLE_PAD, QKV_WIDTH)[:, :DEC_SEQ,
                                                                 base + ATTN_WIDTH:base + 3 * ATTN_WIDTH]
            kv_s[g].append(kv.reshape(DEC_BATCH, DEC_SEQ, 2, N_HEADS, HEAD_DIM))
        merged_p, cs_p = _mix_prompt(conv_in_p, gate_p, outs_p, lses_p, conv_w, l, ROW_TILE)
        merged_s, cs_s = _mix_sample(conv_in_s, gate_s, outs_s, lses_s, conv_w, state_conv, l)
        conv_p.append(cs_p)
        conv_s.append(cs_s)
        m_p, m_s = _ws_matmul(
            merged_p, merged_s, [(w_out, (l,))], k=D_MODEL, n_out=D_MODEL, col_off=0, bm=1024, tn=1024,
            out_dtype=F32, swiglu=False, vmem_mib=56, name="w_out")
        xp, h_p = _resid_norm(xp, m_p, norm_g[l, 3], norm_g[l, 4], 1.0, BF16, ROW_TILE)
        xs, h_s = _resid_norm(xs, m_s, norm_g[l, 3], norm_g[l, 4], 1.0, F32, M_SAMPLE)

        f_p, f_s = _ffn(h_p, h_s, ffn_w_gate, ffn_w_up, ffn_w_down, l, 1)
        g_next = norm_g[l + 1, 0] if l + 1 < DEPTH else norm_g[l, 5]
        xp, h_p = _resid_norm(xp, f_p, norm_g[l, 5], g_next, 0.5, BF16, ROW_TILE)
        xs, h_s = _resid_norm(xs, f_s, norm_g[l, 5], g_next, 0.5, F32, M_SAMPLE)

    y_prompt = xp.reshape(BATCH, SEQ, D_MODEL)
    y_sample = xs.reshape(DEC_BATCH, SAMPLE_PAD, D_MODEL)[:, :DEC_SEQ]
    kv128_p, kv512_p, kv2048_p = [jnp.stack(t) for t in kv_p]
    kv128_s, kv512_s, kv2048_s = [jnp.stack(t) for t in kv_s]
    return (y_prompt, y_sample, jnp.stack(conv_p), jnp.stack(conv_s),
            kv128_p, kv128_s, kv512_p, kv512_s, kv2048_p, kv2048_s)
```

```python
import functools
import math

import jax
import jax.numpy as jnp
import numpy as np
from jax import lax
from jax.experimental import pallas as pl
from jax.experimental.pallas import tpu as pltpu

D_MODEL = 4096
BATCH = 4
SEQ = 2048
DEPTH = 2
DEC_BATCH = 8
DEC_SEQ = 4

D_CONV = D_MODEL
CONV_WIDTH = 3
HEAD_DIM = 128
N_HEADS = D_MODEL // HEAD_DIM
ATTN_WIDTH = N_HEADS * HEAD_DIM
GROUPS = ((128, 1), (512, 4), (2048, 16))
N_GROUPS = len(GROUPS)
N_REL = 128
BAND = 128
N_BUCKETS = 32
MAX_DISTANCE = 2048
D_FF = 11008
RMS_EPS = 1e-6
NEG = -1e30
ATTN_SCALE = 1.0 / math.sqrt(HEAD_DIM)
QKV_BASE = 3 * D_CONV
QKV_WIDTH = 3 * N_GROUPS * ATTN_WIDTH
GATE_BASE = QKV_BASE + QKV_WIDTH
IN_WIDTH = GATE_BASE + 2 * D_MODEL

LANES = 128
SUBLANES = 8
SAMPLE_PAD = SUBLANES
M_PROMPT = BATCH * SEQ
M_SAMPLE = DEC_BATCH * SAMPLE_PAD
MIB = 1024 * 1024

F32 = jnp.float32
BF16 = jnp.bfloat16


def _params(vmem_mib, n_axes):
    return pltpu.CompilerParams(
        dimension_semantics=("arbitrary",) * n_axes,
        vmem_limit_bytes=int(vmem_mib * MIB))


def _rms(x):
    return x * lax.rsqrt(jnp.mean(x * x, axis=-1, keepdims=True) + RMS_EPS)


def _rms_cast_kernel(x_ref, g_ref, h_ref):
    h_ref[...] = (_rms(x_ref[...]) * g_ref[...]).astype(h_ref.dtype)


def _rms_cast(x, g, out_dtype, tr):
    rows = x.shape[0]
    return pl.pallas_call(
        _rms_cast_kernel,
        grid=(rows // tr,),
        in_specs=[pl.BlockSpec((tr, D_MODEL), lambda i: (i, 0)),
                  pl.BlockSpec((1, D_MODEL), lambda i: (0, 0))],
        out_specs=pl.BlockSpec((tr, D_MODEL), lambda i: (i, 0)),
        out_shape=jax.ShapeDtypeStruct((rows, D_MODEL), out_dtype),
        compiler_params=_params(32, 1),
        name="rms_cast",
    )(x, g.reshape(1, D_MODEL))


def _resid_norm_kernel(x_ref, f_ref, gp_ref, gn_ref, xo_ref, h_ref, *, scale):
    x_new = x_ref[...] + scale * (_rms(f_ref[...]) * gp_ref[...])
    xo_ref[...] = x_new
    h_ref[...] = (_rms(x_new) * gn_ref[...]).astype(h_ref.dtype)


def _resid_norm(x, f, g_post, g_next, scale, h_dtype, tr):
    rows = x.shape[0]
    row_spec = pl.BlockSpec((tr, D_MODEL), lambda i: (i, 0))
    g_spec = pl.BlockSpec((1, D_MODEL), lambda i: (0, 0))
    return pl.pallas_call(
        functools.partial(_resid_norm_kernel, scale=scale),
        grid=(rows // tr,),
        in_specs=[row_spec, row_spec, g_spec, g_spec],
        out_specs=[row_spec, row_spec],
        out_shape=[jax.ShapeDtypeStruct((rows, D_MODEL), F32),
                   jax.ShapeDtypeStruct((rows, D_MODEL), h_dtype)],
        compiler_params=_params(40, 1),
        name="resid_norm",
    )(x, f, g_post.reshape(1, D_MODEL), g_next.reshape(1, D_MODEL))


def _ws_matmul_kernel(*refs, n_w, nt, ck, swiglu, slab_out, kv_plan):
    a_ref, as_ref = refs[0], refs[1]
    w_refs = refs[2:2 + n_w]
    n_in = 2 + n_w + (len(kv_plan) if kv_plan and kv_plan[0]["aliased"] else 0)
    o_ref, os_ref = refs[n_in], refs[n_in + 1]
    kv_refs = refs[n_in + 2:n_in + 2 + len(kv_plan)]
    wbufs = refs[n_in + 2 + len(kv_plan):]
    n = pl.program_id(0)
    m = pl.program_id(1)

    @pl.when(n < nt)
    def _cast():
        row = pl.multiple_of(m * ck, 16)
        for w_ref, wb in zip(w_refs, wbufs):
            wb[n % 2, pl.ds(row, ck), :] = w_ref[...].astype(BF16)

    @pl.when(n > 0)
    def _compute():
        slot = (n + 1) % 2

        def compute(x):
            ys = [jnp.dot(x, wb[slot], preferred_element_type=F32) for wb in wbufs]
            if swiglu:
                gate, up = ys
                return gate * jax.nn.sigmoid(gate) * up
            return ys[0]

        res = compute(a_ref[...]).astype(o_ref.dtype)
        if slab_out:
            for j in range(o_ref.shape[0]):
                o_ref[j] = res[:, j * LANES:(j + 1) * LANES]
        else:
            o_ref[...] = res

        for kv_ref, plan in zip(kv_refs, kv_plan):
            tile = n - 1 - plan["first_tile"]
            hit = jnp.logical_and(tile >= 0, tile < plan["n_tiles"])
            if plan["per_batch"]:
                hit = jnp.logical_and(hit, m % plan["tiles_per_batch"] == plan["tiles_per_batch"] - 1)
            rows = kv_ref.shape[0]

            @pl.when(hit)
            def _kv(kv_ref=kv_ref, rows=rows):
                for j in range(kv_ref.shape[1]):
                    kv_ref[:, j, :] = res[res.shape[0] - rows:, j * LANES:(j + 1) * LANES]

        @pl.when(m == 0)
        def _sample():
            os_ref[...] = compute(as_ref[...].astype(BF16)).astype(os_ref.dtype)


def _kv_plan(layer, prev, bm, tn, mt):
    heads_per_tile = tn // HEAD_DIM
    tiles_per_seg = ATTN_WIDTH // tn
    tiles_per_batch = SEQ // bm
    plan = []
    for g, (window, _) in enumerate(GROUPS):
        keep = min(window, SEQ)
        per_batch = keep < SEQ
        rows = keep if per_batch else bm
        assert rows <= bm
        first_tile = (3 * g + 1) * tiles_per_seg
        n_tiles = 2 * tiles_per_seg
        last_row_block = (BATCH if per_batch else mt) - 1

        def index_map(n, m, first_tile=first_tile, n_tiles=n_tiles, per_batch=per_batch,
                      last_row_block=last_row_block):
            tile = n - 1 - first_tile
            inside = m // tiles_per_batch if per_batch else m
            row_block = jnp.where(tile < 0, 0, jnp.where(tile >= n_tiles, last_row_block, inside))
            tile = jnp.clip(tile, 0, n_tiles - 1)
            return (layer, row_block, tile // tiles_per_seg, tile % tiles_per_seg, 0)

        plan.append(dict(
            first_tile=first_tile, n_tiles=n_tiles, per_batch=per_batch,
            tiles_per_batch=tiles_per_batch, aliased=prev is not None,
            spec=pl.BlockSpec((None, rows, None, heads_per_tile, HEAD_DIM), index_map),
            shape=jax.ShapeDtypeStruct((DEPTH, BATCH * keep, 2, N_HEADS, HEAD_DIM), F32)))
    return plan


def _ws_matmul(a, a_s, weights, *, k, n_out, col_off, bm, tn, out_dtype, swiglu, vmem_mib, name,
               slab_out=False, kv_out=None):
    m_rows = a.shape[0]
    mt = m_rows // bm
    nt = n_out // tn
    ck = k // mt
    assert mt * bm == m_rows and nt * tn == n_out and ck * mt == k and ck % 16 == 0
    coff = col_off // tn
    assert coff * tn == col_off
    ms = a_s.shape[0]

    def a_map(n, m):
        return (jnp.where(n == 0, 0, m), 0)

    def w_map(prefix):
        def f(n, m):
            return prefix + (jnp.where(n == nt, mt - 1, m), jnp.minimum(n, nt - 1) + coff)
        return f

    def os_map(n, m):
        return (0, jnp.maximum(n - 1, 0))

    if slab_out:
        o_spec = pl.BlockSpec((tn // LANES, bm, LANES),
                              lambda n, m: (jnp.maximum(n - 1, 0), jnp.where(n == 0, 0, m), 0))
        o_shape = jax.ShapeDtypeStruct((n_out // LANES, m_rows, LANES), out_dtype)
    else:
        o_spec = pl.BlockSpec((bm, tn),
                              lambda n, m: (jnp.where(n == 0, 0, m), jnp.maximum(n - 1, 0)))
        o_shape = jax.ShapeDtypeStruct((m_rows, n_out), out_dtype)

    in_specs = [pl.BlockSpec((bm, k), a_map),
                pl.BlockSpec((ms, k), lambda n, m: (0, 0))]
    for arr, prefix in weights:
        in_specs.append(pl.BlockSpec((None,) * len(prefix) + (ck, tn), w_map(tuple(prefix))))
    operands = [a, a_s] + [arr for arr, _ in weights]
    out_specs = [o_spec, pl.BlockSpec((ms, tn), os_map)]
    out_shape = [o_shape, jax.ShapeDtypeStruct((ms, n_out), F32)]
    kv_plan, aliases = [], {}
    if kv_out is not None:
        layer, prev = kv_out
        kv_plan = _kv_plan(layer, prev, bm, tn, mt)
        if prev is not None:
            for i, arr in enumerate(prev):
                aliases[len(operands)] = len(out_shape) + i
                operands.append(arr)
                in_specs.append(pl.BlockSpec(memory_space=pl.ANY))
        out_specs += [p["spec"] for p in kv_plan]
        out_shape += [p["shape"] for p in kv_plan]
    kernel_plan = [{key: p[key] for key in ("first_tile", "n_tiles", "per_batch", "tiles_per_batch",
                                            "aliased")} for p in kv_plan]
    return pl.pallas_call(
        functools.partial(_ws_matmul_kernel, n_w=len(weights), nt=nt, ck=ck, swiglu=swiglu,
                          slab_out=slab_out, kv_plan=kernel_plan),
        grid=(nt + 1, mt),
        in_specs=in_specs,
        out_specs=out_specs,
        out_shape=out_shape,
        input_output_aliases=aliases,
        scratch_shapes=[pltpu.VMEM((2, k, tn), BF16) for _ in weights],
        compiler_params=_params(vmem_mib, 2),
        name=name,
    )(*operands)


def _t5_bucket(dist):
    dist = np.asarray(dist)
    max_exact = N_BUCKETS // 2
    large = max_exact + (np.log(np.maximum(dist, max_exact) / max_exact)
                         / np.log(MAX_DISTANCE / max_exact)
                         * (N_BUCKETS - max_exact)).astype(np.int32)
    large = np.minimum(large, N_BUCKETS - 1)
    return np.where(dist < max_exact, dist, large).astype(np.int32)


def _group_bias(rel_bias, g, dil):
    buckets = _t5_bucket(np.arange(N_REL + 1) * dil)
    return rel_bias[buckets][:, g * N_HEADS:(g + 1) * N_HEADS].T.astype(F32)


def _neg(*shape):
    return jnp.full(shape, NEG, F32)


def _band_bias(bias_g):
    period = 3 * BAND
    diag = jnp.concatenate([_neg(N_HEADS, BAND - 1), bias_g[:, ::-1],
                            _neg(N_HEADS, period - 2 * BAND)], axis=1)
    flat = jnp.tile(diag, (1, BAND))[:, :BAND * (period - 1)]
    return flat.reshape(N_HEADS, BAND, period - 1)[:, :, BAND - 1:3 * BAND - 1]


def _sample_bias(bias_g, dil):
    rev = bias_g[:, ::-1]
    cached, new = [], []
    for t in range(DEC_SEQ):
        shift = t if dil == 1 else 0
        row = jnp.concatenate([_neg(N_HEADS, shift), rev[:, :N_REL - shift]], axis=1)
        cached.append(row.T)
        cols = []
        for c in range(SAMPLE_PAD):
            valid = c < DEC_SEQ and c <= t and (t - c) % dil == 0
            cols.append(bias_g[:, (t - c) // dil] if valid else _neg(N_HEADS))
        new.append(jnp.stack(cols))
    bc = jnp.stack(cached)
    bn = jnp.stack(new)
    return (jnp.broadcast_to(bc[..., None], bc.shape + (LANES,)),
            jnp.broadcast_to(bn[..., None], bn.shape + (LANES,)))


def _attn_prompt_kernel(q_ref, k_ref, v_ref, b_ref, o_ref, lse_ref, *, hb, dil, unroll):
    hblk = pl.program_id(1)
    nb = SEQ // dil // BAND
    lane = lax.broadcasted_iota(jnp.int32, (BAND, LANES), 1)
    col = lax.broadcasted_iota(jnp.int32, (BAND, 2 * BAND), 1)

    @pl.when(hblk == 0)
    def _init():
        lse_ref[...] = jnp.zeros_like(lse_ref)

    def rows_at(row0):
        if dil == 1:
            return pl.ds(pl.multiple_of(row0, BAND), BAND)
        return pl.ds(row0, BAND, stride=dil)

    def trip(t, carry):
        chains = []
        for u in range(unroll):
            idx = t * unroll + u
            if nb == 1:
                row0 = idx
                mask = None
            else:
                beta = lax.rem(idx, nb)
                row0 = lax.div(idx, nb) + beta * (BAND * dil)
                first = beta == 0
                prev = rows_at(jnp.where(first, row0, row0 - BAND * dil))
                mask = jnp.logical_and(first, col < BAND)
            cur = rows_at(row0)
            for h in range(hb):
                q = q_ref[h, cur, :].astype(BF16)
                k2 = k_ref[h, cur, :].astype(BF16)
                v2 = v_ref[h, cur, :].astype(BF16)
                if nb == 1:
                    bias = b_ref[h][:, BAND:]
                else:
                    bias = b_ref[h]
                    k2 = jnp.concatenate([k_ref[h, prev, :].astype(BF16), k2], axis=0)
                    v2 = jnp.concatenate([v_ref[h, prev, :].astype(BF16), v2], axis=0)
                s = lax.dot_general(q, k2, (((1,), (1,)), ((), ())), preferred_element_type=F32)
                chains.append(dict(u=u, h=h, cur=cur, s=s, bias=bias, mask=mask, v2=v2))
        for c in chains:
            s = c["s"] * ATTN_SCALE + c["bias"]
            if c["mask"] is not None:
                s = jnp.where(c["mask"], NEG, s)
            mx = jnp.max(s, axis=-1, keepdims=True)
            p = jnp.exp(s - mx)
            den = jnp.sum(p, axis=-1, keepdims=True)
            c["p"] = (p / den).astype(BF16)
            c["lse"] = mx + jnp.log(den)
        for c in chains:
            c["o"] = jnp.dot(c["p"], c["v2"], preferred_element_type=F32)
        for u in range(unroll):
            mine = [c for c in chains if c["u"] == u]
            cur = mine[0]["cur"]
            lse_acc = lse_ref[cur, :]
            for c in mine:
                o_ref[c["h"], cur, :] = c["o"]
                lse_acc = jnp.where(lane == hblk * hb + c["h"], c["lse"], lse_acc)
            lse_ref[cur, :] = lse_acc
        return carry

    lax.fori_loop(0, dil * nb // unroll, trip, 0)


def _attn_prompt(qkv, bias_band, g, hb, unroll):
    _, dil = GROUPS[g]

    def slab_map(which):
        first = (3 * g + which) * N_HEADS // hb
        return lambda b, h: (first + h, b, 0)

    blk = (hb, SEQ, LANES)
    return pl.pallas_call(
        functools.partial(_attn_prompt_kernel, hb=hb, dil=dil, unroll=unroll),
        grid=(BATCH, N_HEADS // hb),
        in_specs=[pl.BlockSpec(blk, slab_map(0)),
                  pl.BlockSpec(blk, slab_map(1)),
                  pl.BlockSpec(blk, slab_map(2)),
                  pl.BlockSpec((hb, BAND, 2 * BAND), lambda b, h: (h, 0, 0))],
        out_specs=[pl.BlockSpec(blk, lambda b, h: (h, b, 0)),
                   pl.BlockSpec((SEQ, LANES), lambda b, h: (b, 0))],
        out_shape=[jax.ShapeDtypeStruct((N_HEADS, M_PROMPT, LANES), F32),
                   jax.ShapeDtypeStruct((M_PROMPT, LANES), F32)],
        compiler_params=_params(40, 2),
        name=f"attn_prompt_g{g}",
    )(qkv, qkv, qkv, bias_band)


def _attn_sample_kernel(q_ref, kn_ref, vn_ref, kv_ref, bc_ref, bn_ref, o_ref, lse_ref):
    t = pl.program_id(1)

    @pl.when(t >= DEC_SEQ)
    def _pad_rows():
        o_ref[...] = jnp.zeros_like(o_ref)
        lse_ref[...] = jnp.zeros_like(lse_ref)

    @pl.when(t < DEC_SEQ)
    def _query():
        for ht in range(N_HEADS // SUBLANES):
            hs = slice(ht * SUBLANES, (ht + 1) * SUBLANES)
            vs = slice(N_HEADS + ht * SUBLANES, N_HEADS + (ht + 1) * SUBLANES)
            q = q_ref[hs, :][None]
            s_c = jnp.sum(kv_ref[:, hs, :] * q, axis=-1, keepdims=True)
            s_c = s_c * ATTN_SCALE + bc_ref[:, hs, 0:1]
            s_n = jnp.sum(kn_ref[:, hs, :] * q, axis=-1, keepdims=True)
            s_n = s_n * ATTN_SCALE + bn_ref[:, hs, 0:1]
            mx = jnp.maximum(jnp.max(s_c, axis=0, keepdims=True), jnp.max(s_n, axis=0, keepdims=True))
            p_c = jnp.exp(s_c - mx)
            p_n = jnp.exp(s_n - mx)
            den = jnp.sum(p_c, axis=0, keepdims=True) + jnp.sum(p_n, axis=0, keepdims=True)
            o = (jnp.sum((p_c / den) * kv_ref[:, vs, :], axis=0)
                 + jnp.sum((p_n / den) * vn_ref[:, hs, :], axis=0))
            o_ref[hs, :] = o
            lse_ref[hs, :] = jnp.broadcast_to((mx + jnp.log(den))[0], (SUBLANES, LANES))


def _attn_sample(qkv_s, cache, layer, bias_c, bias_n, g):
    _, dil = GROUPS[g]
    cache_r = cache.reshape(DEPTH, DEC_BATCH, N_REL, dil * 2 * N_HEADS, HEAD_DIM)
    last = DEC_SEQ - 1

    def residue(t):
        return jnp.minimum(t, last) if dil > 1 else 0

    head_blk = (None, None, N_HEADS, HEAD_DIM)
    o, lse = pl.pallas_call(
        _attn_sample_kernel,
        grid=(DEC_BATCH, SAMPLE_PAD),
        in_specs=[pl.BlockSpec((None, None, None, N_HEADS, HEAD_DIM), lambda b, t: (b, t, 3 * g, 0, 0)),
                  pl.BlockSpec((None, SAMPLE_PAD, None, N_HEADS, HEAD_DIM),
                               lambda b, t: (b, 0, 3 * g + 1, 0, 0)),
                  pl.BlockSpec((None, SAMPLE_PAD, None, N_HEADS, HEAD_DIM),
                               lambda b, t: (b, 0, 3 * g + 2, 0, 0)),
                  pl.BlockSpec((None, None, N_REL, 2 * N_HEADS, HEAD_DIM),
                               lambda b, t: (layer, b, 0, residue(t), 0)),
                  pl.BlockSpec((None, N_REL, N_HEADS, LANES), lambda b, t: (jnp.minimum(t, last), 0, 0, 0)),
                  pl.BlockSpec((None, SAMPLE_PAD, N_HEADS, LANES),
                               lambda b, t: (jnp.minimum(t, last), 0, 0, 0))],
        out_specs=[pl.BlockSpec(head_blk, lambda b, t: (b, t, 0, 0)),
                   pl.BlockSpec(head_blk, lambda b, t: (b, t, 0, 0))],
        out_shape=[jax.ShapeDtypeStruct((DEC_BATCH, SAMPLE_PAD, N_HEADS, HEAD_DIM), F32),
                   jax.ShapeDtypeStruct((DEC_BATCH, SAMPLE_PAD, N_HEADS, LANES), F32)],
        compiler_params=_params(40, 2),
        name=f"attn_sample_g{g}",
    )(qkv_s, qkv_s, qkv_s, cache_r, bias_c, bias_n)
    lse_rows = jnp.pad(lse[..., 0].reshape(M_SAMPLE, N_HEADS), ((0, 0), (0, LANES - N_HEADS)))
    return o.reshape(M_SAMPLE, ATTN_WIDTH), lse_rows


def _mix_kernel(*refs, tr, sample):
    (bg_ref, cg_ref, xc_ref, gc_ref, ga_ref, o0_ref, o1_ref, o2_ref,
     l0_ref, l1_ref, l2_ref, cw_ref) = refs[:12]
    if sample:
        prev_ref, merged_ref, cs_ref = refs[12:]
        n_valid = DEC_SEQ
    else:
        merged_ref, cs_ref, carry_ref = refs[12:]
        prev_ref = carry_ref
        n_valid = tr

        @pl.when(pl.program_id(1) == 0)
        def _zero():
            carry_ref[...] = jnp.zeros_like(carry_ref)

    l0, l1, l2 = l0_ref[...], l1_ref[...], l2_ref[...]
    mx = jnp.maximum(jnp.maximum(l0, l1), l2)
    e0, e1, e2 = jnp.exp(l0 - mx), jnp.exp(l1 - mx), jnp.exp(l2 - mx)
    den = e0 + e1 + e2
    a0, a1, a2 = e0 / den, e1 / den, e2 / den
    row = lax.broadcasted_iota(jnp.int32, (tr, HEAD_DIM), 0)

    for h in range(N_HEADS):
        sl = slice(h * HEAD_DIM, (h + 1) * HEAD_DIM)

        def head_out(o_ref):
            return o_ref[:, sl] if sample else o_ref[h]

        cx = cg_ref[:, sl].astype(F32) * xc_ref[:, sl].astype(F32)
        c0 = prev_ref[0:1, sl]
        c1 = prev_ref[1:2, sl]
        cx1 = jnp.where(row == 0, c1, pltpu.roll(cx, 1, 0))
        cx2 = jnp.where(row == 0, c0, jnp.where(row == 1, c1, pltpu.roll(cx, 2, 0)))
        y = cw_ref[0:1, sl] * cx2
        y = y + cw_ref[1:2, sl] * cx1
        y = y + cw_ref[2:3, sl] * cx
        y_conv = bg_ref[:, sl].astype(F32) * y
        y_attn = (head_out(o0_ref) * a0[:, h:h + 1]
                  + head_out(o1_ref) * a1[:, h:h + 1]
                  + head_out(o2_ref) * a2[:, h:h + 1])
        merged = (jax.nn.sigmoid(gc_ref[:, sl].astype(F32)) * y_conv
                  + jax.nn.sigmoid(ga_ref[:, sl].astype(F32)) * y_attn)
        merged_ref[:, sl] = merged.astype(merged_ref.dtype)
        state = cx[n_valid - 2:n_valid, :]
        cs_ref[:, sl] = state
        if not sample:
            carry_ref[0:2, sl] = state


def _mix_prompt(u_conv, u_gate, outs, lses, conv_w, layer, tr):
    nrt = SEQ // tr

    def col_spec(j):
        return pl.BlockSpec((tr, D_MODEL), lambda b, r: (b * nrt + r, j))

    slab_spec = pl.BlockSpec((N_HEADS, tr, LANES), lambda b, r: (0, b * nrt + r, 0))
    lse_spec = pl.BlockSpec((tr, LANES), lambda b, r: (b * nrt + r, 0))
    return pl.pallas_call(
        functools.partial(_mix_kernel, tr=tr, sample=False),
        grid=(BATCH, nrt),
        in_specs=[col_spec(0), col_spec(1), col_spec(2), col_spec(0), col_spec(1),
                  slab_spec, slab_spec, slab_spec, lse_spec, lse_spec, lse_spec,
                  pl.BlockSpec((None, CONV_WIDTH, D_CONV), lambda b, r: (layer, 0, 0))],
        out_specs=[col_spec(0),
                   pl.BlockSpec((None, CONV_WIDTH - 1, D_CONV), lambda b, r: (b, 0, 0))],
        out_shape=[jax.ShapeDtypeStruct((M_PROMPT, D_MODEL), BF16),
                   jax.ShapeDtypeStruct((BATCH, CONV_WIDTH - 1, D_CONV), F32)],
        scratch_shapes=[pltpu.VMEM((8, D_CONV), F32)],
        compiler_params=_params(48, 2),
        name="mix_prompt",
    )(u_conv, u_conv, u_conv, u_gate, u_gate, *outs, *lses, conv_w)


def _mix_sample(u_conv, u_gate, outs, lses, conv_w, state_conv, layer):
    tr = SAMPLE_PAD

    def col_spec(j):
        return pl.BlockSpec((tr, D_MODEL), lambda b: (b, j))

    lse_spec = pl.BlockSpec((tr, LANES), lambda b: (b, 0))
    state_spec = pl.BlockSpec((None, None, CONV_WIDTH - 1, D_CONV), lambda b: (layer, b, 0, 0))
    return pl.pallas_call(
        functools.partial(_mix_kernel, tr=tr, sample=True),
        grid=(DEC_BATCH,),
        in_specs=[col_spec(0), col_spec(1), col_spec(2), col_spec(0), col_spec(1),
                  col_spec(0), col_spec(0), col_spec(0), lse_spec, lse_spec, lse_spec,
                  pl.BlockSpec((None, CONV_WIDTH, D_CONV), lambda b: (layer, 0, 0)),
                  state_spec],
        out_specs=[col_spec(0),
                   pl.BlockSpec((None, CONV_WIDTH - 1, D_CONV), lambda b: (b, 0, 0))],
        out_shape=[jax.ShapeDtypeStruct((M_SAMPLE, D_MODEL), F32),
                   jax.ShapeDtypeStruct((DEC_BATCH, CONV_WIDTH - 1, D_CONV), F32)],
        compiler_params=_params(40, 1),
        name="mix_sample",
    )(u_conv, u_conv, u_conv, u_gate, u_gate, *outs, *lses, conv_w, state_conv)


ATTN_HB = 2
ATTN_UNROLL = (4, 4, 4)
ROW_TILE = 128


def _ffn(h_p, h_s, w_gate, w_up, w_down, layer, idx):
    act_p, act_s = _ws_matmul(
        h_p, h_s, [(w_gate, (layer, idx)), (w_up, (layer, idx))],
        k=D_MODEL, n_out=D_FF, col_off=0, bm=2048, tn=256, out_dtype=BF16, swiglu=True,
        vmem_mib=56, name="ffn_gate_up")
    return _ws_matmul(
        act_p, act_s, [(w_down, (layer, idx))],
        k=D_FF, n_out=D_MODEL, col_off=0, bm=512, tn=512, out_dtype=F32, swiglu=False,
        vmem_mib=58, name="ffn_down")


def _in_proj(h_p, h_s, w_in, layer, col_off, n_out, out_dtype, slab_out, name, kv_out=None):
    bm = 1024 if kv_out is None else 512
    return _ws_matmul(
        h_p, h_s, [(w_in, (layer,))], k=D_MODEL, n_out=n_out, col_off=col_off, bm=bm, tn=1024,
        out_dtype=out_dtype, swiglu=False, vmem_mib=56, name=name, slab_out=slab_out, kv_out=kv_out)


def kernel(x_prompt, x_sample, state_conv, cache_kv_w128, cache_kv_w512, cache_kv_w2048, rel_bias,
           norm_g, ffn_w_gate, ffn_w_up, ffn_w_down, w_in, conv_w, w_out):
    caches = (cache_kv_w128, cache_kv_w512, cache_kv_w2048)
    xp = x_prompt.reshape(M_PROMPT, D_MODEL)
    xs = jnp.pad(x_sample, ((0, 0), (0, SAMPLE_PAD - DEC_SEQ), (0, 0))).reshape(M_SAMPLE, D_MODEL)

    band_bias, samp_bias = [], []
    for g, (window, dil) in enumerate(GROUPS):
        bias_g = _group_bias(rel_bias, g, dil)
        band_bias.append(_band_bias(bias_g))
        samp_bias.append(_sample_bias(bias_g, dil))

    h_p = _rms_cast(xp, norm_g[0, 0], BF16, ROW_TILE)
    h_s = _rms_cast(xs, norm_g[0, 0], F32, M_SAMPLE)

    conv_p, conv_s = [], []
    kv_p = None
    kv_s = [[] for _ in range(N_GROUPS)]
    for l in range(DEPTH):
        f_p, f_s = _ffn(h_p, h_s, ffn_w_gate, ffn_w_up, ffn_w_down, l, 0)
        xp, h_p = _resid_norm(xp, f_p, norm_g[l, 1], norm_g[l, 2], 0.5, BF16, ROW_TILE)
        xs, h_s = _resid_norm(xs, f_s, norm_g[l, 1], norm_g[l, 2], 0.5, F32, M_SAMPLE)

        conv_in_p, conv_in_s = _in_proj(h_p, h_s, w_in, l, 0, QKV_BASE, BF16, False, "w_in_conv")
        qkv_p, qkv_s, *kv_p = _in_proj(h_p, h_s, w_in, l, QKV_BASE, QKV_WIDTH, F32, True, "w_in_qkv",
                                       kv_out=(l, kv_p))
        gate_p, gate_s = _in_proj(h_p, h_s, w_in, l, GATE_BASE, 2 * D_MODEL, BF16, False, "w_in_gate")

        qkv_sh = qkv_s.reshape(DEC_BATCH, SAMPLE_PAD, 3 * N_GROUPS, N_HEADS, HEAD_DIM)
        outs_p, lses_p, outs_s, lses_s = [], [], [], []
        for g in range(N_GROUPS):
            o, lse = _attn_prompt(qkv_p, band_bias[g], g, ATTN_HB, ATTN_UNROLL[g])
            outs_p.append(o)
            lses_p.append(lse)
            o, lse = _attn_sample(qkv_sh, caches[g], l, samp_bias[g][0], samp_bias[g][1], g)
            outs_s.append(o)
            lses_s.append(lse)
            kv_s[g].append(qkv_sh[:, :DEC_SEQ, 3 * g + 1:3 * g + 3])
        merged_p, cs_p = _mix_prompt(conv_in_p, gate_p, outs_p, lses_p, conv_w, l, ROW_TILE)
        merged_s, cs_s = _mix_sample(conv_in_s, gate_s, outs_s, lses_s, conv_w, state_conv, l)
        conv_p.append(cs_p)
        conv_s.append(cs_s)
        m_p, m_s = _ws_matmul(
            merged_p, merged_s, [(w_out, (l,))], k=D_MODEL, n_out=D_MODEL, col_off=0, bm=1024, tn=1024,
            out_dtype=F32, swiglu=False, vmem_mib=56, name="w_out")
        xp, h_p = _resid_norm(xp, m_p, norm_g[l, 3], norm_g[l, 4], 1.0, BF16, ROW_TILE)
        xs, h_s = _resid_norm(xs, m_s, norm_g[l, 3], norm_g[l, 4], 1.0, F32, M_SAMPLE)

        f_p, f_s = _ffn(h_p, h_s, ffn_w_gate, ffn_w_up, ffn_w_down, l, 1)
        g_next = norm_g[l + 1, 0] if l + 1 < DEPTH else norm_g[l, 5]
        xp, h_p = _resid_norm(xp, f_p, norm_g[l, 5], g_next, 0.5, BF16, ROW_TILE)
        xs, h_s = _resid_norm(xs, f_s, norm_g[l, 5], g_next, 0.5, F32, M_SAMPLE)

    y_prompt = xp.reshape(BATCH, SEQ, D_MODEL)
    y_sample = xs.reshape(DEC_BATCH, SAMPLE_PAD, D_MODEL)[:, :DEC_SEQ]
    kv128_p, kv512_p, kv2048_p = [
        t.reshape(DEPTH, BATCH, min(window, SEQ), 2, N_HEADS, HEAD_DIM)
        for t, (window, _) in zip(kv_p, GROUPS)]
    kv128_s, kv512_s, kv2048_s = [jnp.stack(t) for t in kv_s]
    return (y_prompt, y_sample, jnp.stack(conv_p), jnp.stack(conv_s),
            kv128_p, kv128_s, kv512_p, kv512_s, kv2048_p, kv2048_s)
```

```python
import functools
import math

import jax
import jax.numpy as jnp
import numpy as np
from jax import lax
from jax.experimental import pallas as pl
from jax.experimental.pallas import tpu as pltpu

D_MODEL = 4096
BATCH = 4
SEQ = 2048
DEPTH = 2
DEC_BATCH = 8
DEC_SEQ = 4

D_CONV = D_MODEL
CONV_WIDTH = 3
HEAD_DIM = 128
N_HEADS = D_MODEL // HEAD_DIM
ATTN_WIDTH = N_HEADS * HEAD_DIM
GROUPS = ((128, 1), (512, 4), (2048, 16))
N_GROUPS = len(GROUPS)
N_REL = 128
BAND = 128
N_BUCKETS = 32
MAX_DISTANCE = 2048
D_FF = 11008
RMS_EPS = 1e-6
NEG = -1e30
ATTN_SCALE = 1.0 / math.sqrt(HEAD_DIM)
QKV_BASE = 3 * D_CONV
QKV_WIDTH = 3 * N_GROUPS * ATTN_WIDTH
GATE_BASE = QKV_BASE + QKV_WIDTH
IN_WIDTH = GATE_BASE + 2 * D_MODEL

LANES = 128
SUBLANES = 8
SAMPLE_PAD = SUBLANES
M_PROMPT = BATCH * SEQ
M_SAMPLE = DEC_BATCH * SAMPLE_PAD
MIB = 1024 * 1024

F32 = jnp.float32
BF16 = jnp.bfloat16


def _params(vmem_mib, n_axes):
    return pltpu.CompilerParams(
        dimension_semantics=("arbitrary",) * n_axes,
        vmem_limit_bytes=int(vmem_mib * MIB))


def _rms(x):
    return x * lax.rsqrt(jnp.mean(x * x, axis=-1, keepdims=True) + RMS_EPS)


def _rms_cast_kernel(x_ref, g_ref, h_ref):
    h_ref[...] = (_rms(x_ref[...]) * g_ref[...]).astype(h_ref.dtype)


def _rms_cast(x, g, out_dtype, tr):
    rows = x.shape[0]
    return pl.pallas_call(
        _rms_cast_kernel,
        grid=(rows // tr,),
        in_specs=[pl.BlockSpec((tr, D_MODEL), lambda i: (i, 0)),
                  pl.BlockSpec((1, D_MODEL), lambda i: (0, 0))],
        out_specs=pl.BlockSpec((tr, D_MODEL), lambda i: (i, 0)),
        out_shape=jax.ShapeDtypeStruct((rows, D_MODEL), out_dtype),
        compiler_params=_params(32, 1),
        name="rms_cast",
    )(x, g.reshape(1, D_MODEL))


def _resid_norm_kernel(x_ref, f_ref, gp_ref, gn_ref, xo_ref, *h_ref, scale):
    x_new = x_ref[...] + scale * (_rms(f_ref[...]) * gp_ref[...])
    xo_ref[...] = x_new
    for ref in h_ref:
        ref[...] = (_rms(x_new) * gn_ref[...]).astype(ref.dtype)


def _resid_norm(x, f, g_post, g_next, scale, h_dtype, tr):
    rows = x.shape[0]
    row_spec = pl.BlockSpec((tr, D_MODEL), lambda i: (i, 0))
    g_spec = pl.BlockSpec((1, D_MODEL), lambda i: (0, 0))
    emit_h = g_next is not None
    outs = pl.pallas_call(
        functools.partial(_resid_norm_kernel, scale=scale),
        grid=(rows // tr,),
        in_specs=[row_spec, row_spec, g_spec, g_spec],
        out_specs=[row_spec, row_spec][:1 + emit_h],
        out_shape=[jax.ShapeDtypeStruct((rows, D_MODEL), F32),
                   jax.ShapeDtypeStruct((rows, D_MODEL), h_dtype)][:1 + emit_h],
        compiler_params=_params(40, 1),
        name="resid_norm",
    )(x, f, g_post.reshape(1, D_MODEL), (g_next if emit_h else g_post).reshape(1, D_MODEL))
    return (outs[0], outs[1]) if emit_h else (outs[0], None)


SWIGLU_ROWS = 512
KV_STAGE_PAD = SUBLANES


def _ws_matmul_kernel(*refs, n_w, nt, ck, swiglu, slab_out, kv_plan):
    a_ref, as_ref = refs[0], refs[1]
    w_refs = refs[2:2 + n_w]
    n_in = 2 + n_w + (len(kv_plan) if kv_plan and kv_plan[0]["aliased"] else 0)
    o_ref, os_ref = refs[n_in], refs[n_in + 1]
    kv_refs = refs[n_in + 2:n_in + 2 + len(kv_plan)]
    scratch = refs[n_in + 2 + len(kv_plan):]
    even_bufs, odd_bufs = scratch[:n_w], scratch[n_w:2 * n_w]
    stage_ref = scratch[2 * n_w] if kv_plan else None
    n = pl.program_id(0)
    m = pl.program_id(1)

    def cast_into(bufs):
        row = pl.multiple_of(m * ck, 16)
        for w_ref, wb in zip(w_refs, bufs):
            wb[pl.ds(row, ck), :] = w_ref[...].astype(BF16)

    def multiply_with(bufs):
        def compute(x):
            ys = [jnp.dot(x, wb[...], preferred_element_type=F32) for wb in bufs]
            if swiglu:
                gate, up = ys
                return gate * jax.nn.sigmoid(gate) * up
            return ys[0]

        if swiglu:
            res = None
            for r0 in range(0, a_ref.shape[0], SWIGLU_ROWS):
                rows = pl.ds(r0, SWIGLU_ROWS)
                o_ref[rows, :] = compute(a_ref[rows, :]).astype(o_ref.dtype)
        else:
            res = compute(a_ref[...]).astype(o_ref.dtype)
            if slab_out:
                for j in range(o_ref.shape[0]):
                    o_ref[j] = res[:, j * LANES:(j + 1) * LANES]
            else:
                o_ref[...] = res

        for kv_ref, plan in zip(kv_refs, kv_plan):
            tile = n - 1 - plan["first_tile"]
            hit = jnp.logical_and(tile >= 0, tile < plan["n_tiles"])
            if plan["per_batch"]:
                hit = jnp.logical_and(hit, m % plan["tiles_per_batch"] == plan["tiles_per_batch"] - 1)
            rows, heads = kv_ref.shape[0], kv_ref.shape[1]
            pitch = res.shape[0] + KV_STAGE_PAD

            @pl.when(hit)
            def _kv(kv_ref=kv_ref, rows=rows, heads=heads, pitch=pitch):
                for j in range(heads):
                    stage_ref[pl.ds(j * pitch, rows), :] = res[res.shape[0] - rows:,
                                                               j * LANES:(j + 1) * LANES]

                def gather(r, carry):
                    kv_ref[r] = stage_ref[pl.ds(r, heads, stride=pitch), :]
                    return carry

                lax.fori_loop(0, rows, gather, 0, unroll=8)

        @pl.when(m == 0)
        def _sample():
            os_ref[...] = compute(as_ref[...].astype(BF16)).astype(os_ref.dtype)

    middle = jnp.logical_and(n > 0, n < nt)
    last_bufs = even_bufs if (nt - 1) % 2 == 0 else odd_bufs

    @pl.when(n == 0)
    def _first_pass():
        cast_into(even_bufs)

    @pl.when(jnp.logical_and(middle, n % 2 == 1))
    def _odd_pass():
        cast_into(odd_bufs)
        multiply_with(even_bufs)

    @pl.when(jnp.logical_and(middle, n % 2 == 0))
    def _even_pass():
        cast_into(even_bufs)
        multiply_with(odd_bufs)

    @pl.when(n == nt)
    def _last_pass():
        multiply_with(last_bufs)


def _kv_plan(layer, prev, bm, tn, mt):
    heads_per_tile = tn // HEAD_DIM
    tiles_per_seg = ATTN_WIDTH // tn
    tiles_per_batch = SEQ // bm
    plan = []
    for g, (window, _) in enumerate(GROUPS):
        keep = min(window, SEQ)
        per_batch = keep < SEQ
        rows = keep if per_batch else bm
        assert rows <= bm
        first_tile = (3 * g + 1) * tiles_per_seg
        n_tiles = 2 * tiles_per_seg
        last_row_block = (BATCH if per_batch else mt) - 1

        def index_map(n, m, first_tile=first_tile, n_tiles=n_tiles, per_batch=per_batch,
                      last_row_block=last_row_block):
            tile = n - 1 - first_tile
            inside = m // tiles_per_batch if per_batch else m
            row_block = jnp.where(tile < 0, 0, jnp.where(tile >= n_tiles, last_row_block, inside))
            tile = jnp.clip(tile, 0, n_tiles - 1)
            return (layer, row_block, tile // tiles_per_seg, tile % tiles_per_seg, 0)

        plan.append(dict(
            first_tile=first_tile, n_tiles=n_tiles, per_batch=per_batch,
            tiles_per_batch=tiles_per_batch, aliased=prev is not None,
            spec=pl.BlockSpec((None, rows, None, heads_per_tile, HEAD_DIM), index_map),
            shape=jax.ShapeDtypeStruct((DEPTH, BATCH * keep, 2, N_HEADS, HEAD_DIM), F32)))
    return plan


def _ws_matmul(a, a_s, weights, *, k, n_out, col_off, bm, tn, out_dtype, swiglu, vmem_mib, name,
               slab_out=False, kv_out=None):
    m_rows = a.shape[0]
    mt = m_rows // bm
    nt = n_out // tn
    ck = k // mt
    assert mt * bm == m_rows and nt * tn == n_out and ck * mt == k and ck % 16 == 0
    coff = col_off // tn
    assert coff * tn == col_off
    ms = a_s.shape[0]

    def a_map(n, m):
        return (jnp.where(n == 0, 0, m), 0)

    def w_map(prefix):
        def f(n, m):
            return prefix + (jnp.where(n == nt, mt - 1, m), jnp.minimum(n, nt - 1) + coff)
        return f

    def os_map(n, m):
        return (0, jnp.maximum(n - 1, 0))

    if slab_out:
        o_spec = pl.BlockSpec((tn // LANES, bm, LANES),
                              lambda n, m: (jnp.maximum(n - 1, 0), jnp.where(n == 0, 0, m), 0))
        o_shape = jax.ShapeDtypeStruct((n_out // LANES, m_rows, LANES), out_dtype)
    else:
        o_spec = pl.BlockSpec((bm, tn),
                              lambda n, m: (jnp.where(n == 0, 0, m), jnp.maximum(n - 1, 0)))
        o_shape = jax.ShapeDtypeStruct((m_rows, n_out), out_dtype)

    in_specs = [pl.BlockSpec((bm, k), a_map),
                pl.BlockSpec((ms, k), lambda n, m: (0, 0))]
    for arr, prefix in weights:
        in_specs.append(pl.BlockSpec((None,) * len(prefix) + (ck, tn), w_map(tuple(prefix))))
    operands = [a, a_s] + [arr for arr, _ in weights]
    out_specs = [o_spec, pl.BlockSpec((ms, tn), os_map)]
    out_shape = [o_shape, jax.ShapeDtypeStruct((ms, n_out), F32)]
    kv_plan, aliases = [], {}
    if kv_out is not None:
        layer, prev = kv_out
        kv_plan = _kv_plan(layer, prev, bm, tn, mt)
        if prev is not None:
            for i, arr in enumerate(prev):
                aliases[len(operands)] = len(out_shape) + i
                operands.append(arr)
                in_specs.append(pl.BlockSpec(memory_space=pl.ANY))
        out_specs += [p["spec"] for p in kv_plan]
        out_shape += [p["shape"] for p in kv_plan]
    kernel_plan = [{key: p[key] for key in ("first_tile", "n_tiles", "per_batch", "tiles_per_batch",
                                            "aliased")} for p in kv_plan]
    scratch_shapes = [pltpu.VMEM((k, tn), BF16) for _ in range(2 * len(weights))]
    if kv_plan:
        scratch_shapes.append(pltpu.VMEM(((tn // HEAD_DIM) * (bm + KV_STAGE_PAD), LANES), F32))
    return pl.pallas_call(
        functools.partial(_ws_matmul_kernel, n_w=len(weights), nt=nt, ck=ck, swiglu=swiglu,
                          slab_out=slab_out, kv_plan=kernel_plan),
        grid=(nt + 1, mt),
        in_specs=in_specs,
        out_specs=out_specs,
        out_shape=out_shape,
        input_output_aliases=aliases,
        scratch_shapes=scratch_shapes,
        compiler_params=_params(vmem_mib, 2),
        name=name,
    )(*operands)


def _t5_bucket(dist):
    dist = np.asarray(dist)
    max_exact = N_BUCKETS // 2
    large = max_exact + (np.log(np.maximum(dist, max_exact) / max_exact)
                         / np.log(MAX_DISTANCE / max_exact)
                         * (N_BUCKETS - max_exact)).astype(np.int32)
    large = np.minimum(large, N_BUCKETS - 1)
    return np.where(dist < max_exact, dist, large).astype(np.int32)


def _group_bias(rel_bias, g, dil):
    buckets = _t5_bucket(np.arange(N_REL + 1) * dil)
    return rel_bias[buckets][:, g * N_HEADS:(g + 1) * N_HEADS].T.astype(F32)


def _neg(*shape):
    return jnp.full(shape, NEG, F32)


def _band_bias(bias_g):
    period = 3 * BAND
    diag = jnp.concatenate([_neg(N_HEADS, BAND - 1), bias_g[:, ::-1],
                            _neg(N_HEADS, period - 2 * BAND)], axis=1)
    flat = jnp.tile(diag, (1, BAND))[:, :BAND * (period - 1)]
    return flat.reshape(N_HEADS, BAND, period - 1)[:, :, BAND - 1:3 * BAND - 1]


def _sample_bias(bias_g, dil):
    rev = bias_g[:, ::-1]
    cached, new = [], []
    for t in range(DEC_SEQ):
        shift = t if dil == 1 else 0
        if t == 0 or dil == 1:
            row = jnp.concatenate([_neg(N_HEADS, shift), rev[:, :N_REL - shift]], axis=1)
            cached.append(row.T)
        cols = []
        for c in range(SAMPLE_PAD):
            valid = c < DEC_SEQ and c <= t and (t - c) % dil == 0
            cols.append(bias_g[:, (t - c) // dil] if valid else _neg(N_HEADS))
        new.append(jnp.stack(cols))
    bc = jnp.stack(cached)
    bn = jnp.stack(new)
    return (jnp.broadcast_to(bc[..., None], bc.shape + (LANES,)),
            jnp.broadcast_to(bn[..., None], bn.shape + (LANES,)))


def _attn_prompt_kernel(q_ref, k_ref, v_ref, b_ref, o_ref, lse_ref, *, hb, dil, unroll):
    hblk = pl.program_id(1)
    nb = SEQ // dil // BAND
    lane = lax.broadcasted_iota(jnp.int32, (BAND, LANES), 1)
    col = lax.broadcasted_iota(jnp.int32, (BAND, 2 * BAND), 1)

    @pl.when(hblk == 0)
    def _init():
        lse_ref[...] = jnp.zeros_like(lse_ref)

    def rows_at(row0):
        if dil == 1:
            return pl.ds(pl.multiple_of(row0, BAND), BAND)
        return pl.ds(row0, BAND, stride=dil)

    def trip(t, carry):
        chains = []
        for u in range(unroll):
            idx = t * unroll + u
            if nb == 1:
                row0 = idx
                mask = None
            else:
                beta = lax.rem(idx, nb)
                row0 = lax.div(idx, nb) + beta * (BAND * dil)
                first = beta == 0
                prev = rows_at(jnp.where(first, row0, row0 - BAND * dil))
                mask = jnp.logical_and(first, col < BAND)
            cur = rows_at(row0)
            for h in range(hb):
                q = q_ref[h, cur, :].astype(BF16)
                k2 = k_ref[h, cur, :].astype(BF16)
                v2 = v_ref[h, cur, :].astype(BF16)
                if nb == 1:
                    bias = b_ref[h][:, BAND:]
                else:
                    bias = b_ref[h]
                    k2 = jnp.concatenate([k_ref[h, prev, :].astype(BF16), k2], axis=0)
                    v2 = jnp.concatenate([v_ref[h, prev, :].astype(BF16), v2], axis=0)
                s = lax.dot_general(q, k2, (((1,), (1,)), ((), ())), preferred_element_type=F32)
                chains.append(dict(u=u, h=h, cur=cur, s=s, bias=bias, mask=mask, v2=v2))
        for c in chains:
            s = c["s"] * ATTN_SCALE + c["bias"]
            if c["mask"] is not None:
                s = jnp.where(c["mask"], NEG, s)
            mx = jnp.max(s, axis=-1, keepdims=True)
            p = jnp.exp(s - mx)
            den = jnp.sum(p, axis=-1, keepdims=True)
            c["p"] = (p / den).astype(BF16)
            c["lse"] = mx + jnp.log(den)
        for c in chains:
            c["o"] = jnp.dot(c["p"], c["v2"], preferred_element_type=F32)
        for u in range(unroll):
            mine = [c for c in chains if c["u"] == u]
            cur = mine[0]["cur"]
            lse_acc = lse_ref[cur, :]
            for c in mine:
                o_ref[c["h"], cur, :] = c["o"]
                lse_acc = jnp.where(lane == hblk * hb + c["h"], c["lse"], lse_acc)
            lse_ref[cur, :] = lse_acc
        return carry

    lax.fori_loop(0, dil * nb // unroll, trip, 0)


def _attn_prompt(qkv, bias_band, g, hb, unroll):
    _, dil = GROUPS[g]

    def slab_map(which):
        first = (3 * g + which) * N_HEADS // hb
        return lambda b, h: (first + h, b, 0)

    blk = (hb, SEQ, LANES)
    return pl.pallas_call(
        functools.partial(_attn_prompt_kernel, hb=hb, dil=dil, unroll=unroll),
        grid=(BATCH, N_HEADS // hb),
        in_specs=[pl.BlockSpec(blk, slab_map(0)),
                  pl.BlockSpec(blk, slab_map(1)),
                  pl.BlockSpec(blk, slab_map(2)),
                  pl.BlockSpec((hb, BAND, 2 * BAND), lambda b, h: (h, 0, 0))],
        out_specs=[pl.BlockSpec(blk, lambda b, h: (h, b, 0)),
                   pl.BlockSpec((SEQ, LANES), lambda b, h: (b, 0))],
        out_shape=[jax.ShapeDtypeStruct((N_HEADS, M_PROMPT, LANES), F32),
                   jax.ShapeDtypeStruct((M_PROMPT, LANES), F32)],
        compiler_params=_params(40, 2),
        name=f"attn_prompt_g{g}",
    )(qkv, qkv, qkv, bias_band)


def _attn_sample_kernel(q_ref, kn_ref, vn_ref, kv_ref, bc_ref, bn_ref, o_ref, lse_ref):
    t = pl.program_id(1)

    @pl.when(t >= DEC_SEQ)
    def _pad_rows():
        o_ref[...] = jnp.zeros_like(o_ref)
        lse_ref[...] = jnp.zeros_like(lse_ref)

    @pl.when(t < DEC_SEQ)
    def _query():
        for ht in range(N_HEADS // SUBLANES):
            hs = slice(ht * SUBLANES, (ht + 1) * SUBLANES)
            vs = slice(N_HEADS + ht * SUBLANES, N_HEADS + (ht + 1) * SUBLANES)
            q = q_ref[hs, :][None]
            s_c = jnp.sum(kv_ref[:, hs, :] * q, axis=-1, keepdims=True)
            s_c = s_c * ATTN_SCALE + bc_ref[:, hs, 0:1]
            s_n = jnp.sum(kn_ref[:, hs, :] * q, axis=-1, keepdims=True)
            s_n = s_n * ATTN_SCALE + bn_ref[:, hs, 0:1]
            mx = jnp.maximum(jnp.max(s_c, axis=0, keepdims=True), jnp.max(s_n, axis=0, keepdims=True))
            p_c = jnp.exp(s_c - mx)
            p_n = jnp.exp(s_n - mx)
            den = jnp.sum(p_c, axis=0, keepdims=True) + jnp.sum(p_n, axis=0, keepdims=True)
            o = (jnp.sum((p_c / den) * kv_ref[:, vs, :], axis=0)
                 + jnp.sum((p_n / den) * vn_ref[:, hs, :], axis=0))
            o_ref[hs, :] = o
            lse_ref[hs, :] = jnp.broadcast_to((mx + jnp.log(den))[0], (SUBLANES, LANES))


def _attn_sample(qkv_s, cache, layer, bias_c, bias_n, g):
    _, dil = GROUPS[g]
    cache_r = cache.reshape(DEPTH, DEC_BATCH, N_REL, dil * 2 * N_HEADS, HEAD_DIM)
    last = DEC_SEQ - 1

    def residue(t):
        return jnp.minimum(t, last) if dil > 1 else 0

    head_blk = (None, None, N_HEADS, HEAD_DIM)
    o, lse = pl.pallas_call(
        _attn_sample_kernel,
        grid=(DEC_BATCH, SAMPLE_PAD),
        in_specs=[pl.BlockSpec((None, None, None, N_HEADS, HEAD_DIM), lambda b, t: (b, t, 3 * g, 0, 0)),
                  pl.BlockSpec((None, SAMPLE_PAD, None, N_HEADS, HEAD_DIM),
                               lambda b, t: (b, 0, 3 * g + 1, 0, 0)),
                  pl.BlockSpec((None, SAMPLE_PAD, None, N_HEADS, HEAD_DIM),
                               lambda b, t: (b, 0, 3 * g + 2, 0, 0)),
                  pl.BlockSpec((None, None, N_REL, 2 * N_HEADS, HEAD_DIM),
                               lambda b, t: (layer, b, 0, residue(t), 0)),
                  pl.BlockSpec((None, N_REL, N_HEADS, LANES),
                               lambda b, t: (jnp.minimum(t, bias_c.shape[0] - 1), 0, 0, 0)),
                  pl.BlockSpec((None, SAMPLE_PAD, N_HEADS, LANES),
                               lambda b, t: (jnp.minimum(t, last), 0, 0, 0))],
        out_specs=[pl.BlockSpec(head_blk, lambda b, t: (b, t, 0, 0)),
                   pl.BlockSpec(head_blk, lambda b, t: (b, t, 0, 0))],
        out_shape=[jax.ShapeDtypeStruct((DEC_BATCH, SAMPLE_PAD, N_HEADS, HEAD_DIM), F32),
                   jax.ShapeDtypeStruct((DEC_BATCH, SAMPLE_PAD, N_HEADS, LANES), F32)],
        compiler_params=_params(40, 2),
        name=f"attn_sample_g{g}",
    )(qkv_s, qkv_s, qkv_s, cache_r, bias_c, bias_n)
    lse_rows = jnp.pad(lse[..., 0].reshape(M_SAMPLE, N_HEADS), ((0, 0), (0, LANES - N_HEADS)))
    return o.reshape(M_SAMPLE, ATTN_WIDTH), lse_rows


def _mix_kernel(*refs, tr, sample):
    (bg_ref, cg_ref, xc_ref, gc_ref, ga_ref, o0_ref, o1_ref, o2_ref,
     l0_ref, l1_ref, l2_ref, cw_ref) = refs[:12]
    if sample:
        prev_ref, merged_ref, cs_ref = refs[12:]
        n_valid = DEC_SEQ
    else:
        merged_ref, cs_ref, carry_ref = refs[12:]
        prev_ref = carry_ref
        n_valid = tr

        @pl.when(pl.program_id(1) == 0)
        def _zero():
            carry_ref[...] = jnp.zeros_like(carry_ref)

    l0, l1, l2 = l0_ref[...], l1_ref[...], l2_ref[...]
    mx = jnp.maximum(jnp.maximum(l0, l1), l2)
    e0, e1, e2 = jnp.exp(l0 - mx), jnp.exp(l1 - mx), jnp.exp(l2 - mx)
    den = e0 + e1 + e2
    a0, a1, a2 = e0 / den, e1 / den, e2 / den
    row = lax.broadcasted_iota(jnp.int32, (tr, HEAD_DIM), 0)

    for h in range(N_HEADS):
        sl = slice(h * HEAD_DIM, (h + 1) * HEAD_DIM)

        def head_out(o_ref):
            return o_ref[:, sl] if sample else o_ref[h]

        cx = cg_ref[:, sl].astype(F32) * xc_ref[:, sl].astype(F32)
        c0 = prev_ref[0:1, sl]
        c1 = prev_ref[1:2, sl]
        cx1 = jnp.where(row == 0, c1, pltpu.roll(cx, 1, 0))
        cx2 = jnp.where(row == 0, c0, jnp.where(row == 1, c1, pltpu.roll(cx, 2, 0)))
        y = cw_ref[0:1, sl] * cx2
        y = y + cw_ref[1:2, sl] * cx1
        y = y + cw_ref[2:3, sl] * cx
        y_conv = bg_ref[:, sl].astype(F32) * y
        y_attn = (head_out(o0_ref) * a0[:, h:h + 1]
                  + head_out(o1_ref) * a1[:, h:h + 1]
                  + head_out(o2_ref) * a2[:, h:h + 1])
        merged = (jax.nn.sigmoid(gc_ref[:, sl].astype(F32)) * y_conv
                  + jax.nn.sigmoid(ga_ref[:, sl].astype(F32)) * y_attn)
        merged_ref[:, sl] = merged.astype(merged_ref.dtype)
        state = cx[n_valid - 2:n_valid, :]
        cs_ref[:, sl] = state
        if not sample:
            carry_ref[0:2, sl] = state


def _mix_prompt(u_conv, u_gate, outs, lses, conv_w, layer, tr):
    nrt = SEQ // tr

    def col_spec(j):
        return pl.BlockSpec((tr, D_MODEL), lambda b, r: (b * nrt + r, j))

    slab_spec = pl.BlockSpec((N_HEADS, tr, LANES), lambda b, r: (0, b * nrt + r, 0))
    lse_spec = pl.BlockSpec((tr, LANES), lambda b, r: (b * nrt + r, 0))
    return pl.pallas_call(
        functools.partial(_mix_kernel, tr=tr, sample=False),
        grid=(BATCH, nrt),
        in_specs=[col_spec(0), col_spec(1), col_spec(2), col_spec(0), col_spec(1),
                  slab_spec, slab_spec, slab_spec, lse_spec, lse_spec, lse_spec,
                  pl.BlockSpec((None, CONV_WIDTH, D_CONV), lambda b, r: (layer, 0, 0))],
        out_specs=[col_spec(0),
                   pl.BlockSpec((None, CONV_WIDTH - 1, D_CONV), lambda b, r: (b, 0, 0))],
        out_shape=[jax.ShapeDtypeStruct((M_PROMPT, D_MODEL), BF16),
                   jax.ShapeDtypeStruct((BATCH, CONV_WIDTH - 1, D_CONV), F32)],
        scratch_shapes=[pltpu.VMEM((8, D_CONV), F32)],
        compiler_params=_params(48, 2),
        name="mix_prompt",
    )(u_conv, u_conv, u_conv, u_gate, u_gate, *outs, *lses, conv_w)


def _mix_sample(u_conv, u_gate, outs, lses, conv_w, state_conv, layer):
    tr = SAMPLE_PAD

    def col_spec(j):
        return pl.BlockSpec((tr, D_MODEL), lambda b: (b, j))

    lse_spec = pl.BlockSpec((tr, LANES), lambda b: (b, 0))
    state_spec = pl.BlockSpec((None, None, CONV_WIDTH - 1, D_CONV), lambda b: (layer, b, 0, 0))
    return pl.pallas_call(
        functools.partial(_mix_kernel, tr=tr, sample=True),
        grid=(DEC_BATCH,),
        in_specs=[col_spec(0), col_spec(1), col_spec(2), col_spec(0), col_spec(1),
                  col_spec(0), col_spec(0), col_spec(0), lse_spec, lse_spec, lse_spec,
                  pl.BlockSpec((None, CONV_WIDTH, D_CONV), lambda b: (layer, 0, 0)),
                  state_spec],
        out_specs=[col_spec(0),
                   pl.BlockSpec((None, CONV_WIDTH - 1, D_CONV), lambda b: (b, 0, 0))],
        out_shape=[jax.ShapeDtypeStruct((M_SAMPLE, D_MODEL), F32),
                   jax.ShapeDtypeStruct((DEC_BATCH, CONV_WIDTH - 1, D_CONV), F32)],
        compiler_params=_params(40, 1),
        name="mix_sample",
    )(u_conv, u_conv, u_conv, u_gate, u_gate, *outs, *lses, conv_w, state_conv)


ATTN_HB = 4
ATTN_UNROLL = (2, 2, 2)
ROW_TILE = 128


def _ffn(h_p, h_s, w_gate, w_up, w_down, layer, idx):
    act_p, act_s = _ws_matmul(
        h_p, h_s, [(w_gate, (layer, idx)), (w_up, (layer, idx))],
        k=D_MODEL, n_out=D_FF, col_off=0, bm=2048, tn=256, out_dtype=BF16, swiglu=True,
        vmem_mib=56, name="ffn_gate_up")
    return _ws_matmul(
        act_p, act_s, [(w_down, (layer, idx))],
        k=D_FF, n_out=D_MODEL, col_off=0, bm=512, tn=512, out_dtype=F32, swiglu=False,
        vmem_mib=58, name="ffn_down")


def _in_proj(h_p, h_s, w_in, layer, col_off, n_out, out_dtype, slab_out, name, kv_out=None):
    bm = 1024 if kv_out is None else 512
    return _ws_matmul(
        h_p, h_s, [(w_in, (layer,))], k=D_MODEL, n_out=n_out, col_off=col_off, bm=bm, tn=1024,
        out_dtype=out_dtype, swiglu=False, vmem_mib=56, name=name, slab_out=slab_out, kv_out=kv_out)


def kernel(x_prompt, x_sample, state_conv, cache_kv_w128, cache_kv_w512, cache_kv_w2048, rel_bias,
           norm_g, ffn_w_gate, ffn_w_up, ffn_w_down, w_in, conv_w, w_out):
    caches = (cache_kv_w128, cache_kv_w512, cache_kv_w2048)
    xp = x_prompt.reshape(M_PROMPT, D_MODEL)
    xs = jnp.pad(x_sample, ((0, 0), (0, SAMPLE_PAD - DEC_SEQ), (0, 0))).reshape(M_SAMPLE, D_MODEL)

    band_bias, samp_bias = [], []
    for g, (window, dil) in enumerate(GROUPS):
        bias_g = _group_bias(rel_bias, g, dil)
        band_bias.append(_band_bias(bias_g))
        samp_bias.append(_sample_bias(bias_g, dil))

    h_p = _rms_cast(xp, norm_g[0, 0], BF16, ROW_TILE)
    h_s = _rms_cast(xs, norm_g[0, 0], F32, M_SAMPLE)

    conv_p, conv_s = [], []
    kv_p = None
    kv_s = [[] for _ in range(N_GROUPS)]
    for l in range(DEPTH):
        f_p, f_s = _ffn(h_p, h_s, ffn_w_gate, ffn_w_up, ffn_w_down, l, 0)
        xp, h_p = _resid_norm(xp, f_p, norm_g[l, 1], norm_g[l, 2], 0.5, BF16, ROW_TILE)
        xs, h_s = _resid_norm(xs, f_s, norm_g[l, 1], norm_g[l, 2], 0.5, F32, M_SAMPLE)

        conv_in_p, conv_in_s = _in_proj(h_p, h_s, w_in, l, 0, QKV_BASE, BF16, False, "w_in_conv")
        qkv_p, qkv_s, *kv_p = _in_proj(h_p, h_s, w_in, l, QKV_BASE, QKV_WIDTH, F32, True, "w_in_qkv",
                                       kv_out=(l, kv_p))
        gate_p, gate_s = _in_proj(h_p, h_s, w_in, l, GATE_BASE, 2 * D_MODEL, BF16, False, "w_in_gate")

        qkv_sh = qkv_s.reshape(DEC_BATCH, SAMPLE_PAD, 3 * N_GROUPS, N_HEADS, HEAD_DIM)
        outs_p, lses_p, outs_s, lses_s = [], [], [], []
        for g in range(N_GROUPS):
            o, lse = _attn_prompt(qkv_p, band_bias[g], g, ATTN_HB, ATTN_UNROLL[g])
            outs_p.append(o)
            lses_p.append(lse)
            o, lse = _attn_sample(qkv_sh, caches[g], l, samp_bias[g][0], samp_bias[g][1], g)
            outs_s.append(o)
            lses_s.append(lse)
            kv_s[g].append(qkv_sh[:, :DEC_SEQ, 3 * g + 1:3 * g + 3])
        merged_p, cs_p = _mix_prompt(conv_in_p, gate_p, outs_p, lses_p, conv_w, l, ROW_TILE)
        merged_s, cs_s = _mix_sample(conv_in_s, gate_s, outs_s, lses_s, conv_w, state_conv, l)
        conv_p.append(cs_p)
        conv_s.append(cs_s)
        m_p, m_s = _ws_matmul(
            merged_p, merged_s, [(w_out, (l,))], k=D_MODEL, n_out=D_MODEL, col_off=0, bm=1024, tn=1024,
            out_dtype=F32, swiglu=False, vmem_mib=56, name="w_out")
        xp, h_p = _resid_norm(xp, m_p, norm_g[l, 3], norm_g[l, 4], 1.0, BF16, ROW_TILE)
        xs, h_s = _resid_norm(xs, m_s, norm_g[l, 3], norm_g[l, 4], 1.0, F32, M_SAMPLE)

        f_p, f_s = _ffn(h_p, h_s, ffn_w_gate, ffn_w_up, ffn_w_down, l, 1)
        g_next = norm_g[l + 1, 0] if l + 1 < DEPTH else None
        xp, h_p = _resid_norm(xp, f_p, norm_g[l, 5], g_next, 0.5, BF16, ROW_TILE)
        xs, h_s = _resid_norm(xs, f_s, norm_g[l, 5], g_next, 0.5, F32, M_SAMPLE)

    y_prompt = xp.reshape(BATCH, SEQ, D_MODEL)
    y_sample = xs.reshape(DEC_BATCH, SAMPLE_PAD, D_MODEL)[:, :DEC_SEQ]
    kv128_p, kv512_p, kv2048_p = [
        t.reshape(DEPTH, BATCH, min(window, SEQ), 2, N_HEADS, HEAD_DIM)
        for t, (window, _) in zip(kv_p, GROUPS)]
    kv128_s, kv512_s, kv2048_s = [jnp.stack(t) for t in kv_s]
    return (y_prompt, y_sample, jnp.stack(conv_p), jnp.stack(conv_s),
            kv128_p, kv128_s, kv512_p, kv512_s, kv2048_p, kv2048_s)
```

```python
import functools
import math

import jax
import jax.numpy as jnp
import numpy as np
from jax import lax
from jax.experimental import pallas as pl
from jax.experimental.pallas import tpu as pltpu

D_MODEL = 4096
BATCH = 4
SEQ = 2048
DEPTH = 2
DEC_BATCH = 8
DEC_SEQ = 4

D_CONV = D_MODEL
CONV_WIDTH = 3
HEAD_DIM = 128
N_HEADS = D_MODEL // HEAD_DIM
ATTN_WIDTH = N_HEADS * HEAD_DIM
GROUPS = ((128, 1), (512, 4), (2048, 16))
N_GROUPS = len(GROUPS)
N_REL = 128
BAND = 128
N_BUCKETS = 32
MAX_DISTANCE = 2048
D_FF = 11008
RMS_EPS = 1e-6
NEG = -1e30
ATTN_SCALE = 1.0 / math.sqrt(HEAD_DIM)
QKV_BASE = 3 * D_CONV
QKV_WIDTH = 3 * N_GROUPS * ATTN_WIDTH
GATE_BASE = QKV_BASE + QKV_WIDTH
IN_WIDTH = GATE_BASE + 2 * D_MODEL

LANES = 128
SUBLANES = 8
SAMPLE_PAD = SUBLANES
M_PROMPT = BATCH * SEQ
M_SAMPLE = DEC_BATCH * SAMPLE_PAD
MIB = 1024 * 1024

F32 = jnp.float32
BF16 = jnp.bfloat16


def _params(vmem_mib, n_axes):
    return pltpu.CompilerParams(
        dimension_semantics=("arbitrary",) * n_axes,
        vmem_limit_bytes=int(vmem_mib * MIB))


def _rms(x):
    return x * lax.rsqrt(jnp.mean(x * x, axis=-1, keepdims=True) + RMS_EPS)


def _rms_cast_kernel(x_ref, g_ref, h_ref):
    h_ref[...] = (_rms(x_ref[...]) * g_ref[...]).astype(h_ref.dtype)


def _rms_cast(x, g, out_dtype, tr):
    rows = x.shape[0]
    return pl.pallas_call(
        _rms_cast_kernel,
        grid=(rows // tr,),
        in_specs=[pl.BlockSpec((tr, D_MODEL), lambda i: (i, 0)),
                  pl.BlockSpec((1, D_MODEL), lambda i: (0, 0))],
        out_specs=pl.BlockSpec((tr, D_MODEL), lambda i: (i, 0)),
        out_shape=jax.ShapeDtypeStruct((rows, D_MODEL), out_dtype),
        compiler_params=_params(32, 1),
        name="rms_cast",
    )(x, g.reshape(1, D_MODEL))


def _resid_norm_kernel(x_ref, f_ref, gp_ref, gn_ref, xo_ref, *h_ref, scale):
    x_new = x_ref[...] + scale * (_rms(f_ref[...]) * gp_ref[...])
    xo_ref[...] = x_new
    for ref in h_ref:
        ref[...] = (_rms(x_new) * gn_ref[...]).astype(ref.dtype)


def _resid_norm(x, f, g_post, g_next, scale, h_dtype, tr):
    rows = x.shape[0]
    row_spec = pl.BlockSpec((tr, D_MODEL), lambda i: (i, 0))
    g_spec = pl.BlockSpec((1, D_MODEL), lambda i: (0, 0))
    emit_h = g_next is not None
    outs = pl.pallas_call(
        functools.partial(_resid_norm_kernel, scale=scale),
        grid=(rows // tr,),
        in_specs=[row_spec, row_spec, g_spec, g_spec],
        out_specs=[row_spec, row_spec][:1 + emit_h],
        out_shape=[jax.ShapeDtypeStruct((rows, D_MODEL), F32),
                   jax.ShapeDtypeStruct((rows, D_MODEL), h_dtype)][:1 + emit_h],
        compiler_params=_params(40, 1),
        name="resid_norm",
    )(x, f, g_post.reshape(1, D_MODEL), (g_next if emit_h else g_post).reshape(1, D_MODEL))
    return (outs[0], outs[1]) if emit_h else (outs[0], None)


SWIGLU_ROWS = 512
KV_STAGE_PAD = SUBLANES


def _ws_matmul_kernel(*refs, n_w, nt, ck, swiglu, slab_out, kv_plan):
    a_ref, as_ref = refs[0], refs[1]
    w_refs = refs[2:2 + n_w]
    n_in = 2 + n_w + (len(kv_plan) if kv_plan and kv_plan[0]["aliased"] else 0)
    o_ref, os_ref = refs[n_in], refs[n_in + 1]
    kv_refs = refs[n_in + 2:n_in + 2 + len(kv_plan)]
    scratch = refs[n_in + 2 + len(kv_plan):]
    even_bufs, odd_bufs = scratch[:n_w], scratch[n_w:2 * n_w]
    stage_ref = scratch[2 * n_w] if kv_plan else None
    n = pl.program_id(0)
    m = pl.program_id(1)

    def cast_into(bufs):
        row = pl.multiple_of(m * ck, 16)
        for w_ref, wb in zip(w_refs, bufs):
            wb[pl.ds(row, ck), :] = w_ref[...].astype(BF16)

    def multiply_with(bufs):
        def compute(x):
            ys = [jnp.dot(x, wb[...], preferred_element_type=F32) for wb in bufs]
            if swiglu:
                gate, up = ys
                return gate * jax.nn.sigmoid(gate) * up
            return ys[0]

        if swiglu:
            res = None
            for r0 in range(0, a_ref.shape[0], SWIGLU_ROWS):
                rows = pl.ds(r0, SWIGLU_ROWS)
                o_ref[rows, :] = compute(a_ref[rows, :]).astype(o_ref.dtype)
        else:
            res = compute(a_ref[...]).astype(o_ref.dtype)
            if slab_out:
                for j in range(o_ref.shape[0]):
                    o_ref[j] = res[:, j * LANES:(j + 1) * LANES]
            else:
                o_ref[...] = res

        for kv_ref, plan in zip(kv_refs, kv_plan):
            tile = n - 1 - plan["first_tile"]
            hit = jnp.logical_and(tile >= 0, tile < plan["n_tiles"])
            if plan["per_batch"]:
                hit = jnp.logical_and(hit, m % plan["tiles_per_batch"] == plan["tiles_per_batch"] - 1)
            rows, heads = kv_ref.shape[0], kv_ref.shape[1]
            pitch = res.shape[0] + KV_STAGE_PAD

            @pl.when(hit)
            def _kv(kv_ref=kv_ref, rows=rows, heads=heads, pitch=pitch):
                for j in range(heads):
                    stage_ref[pl.ds(j * pitch, rows), :] = res[res.shape[0] - rows:,
                                                               j * LANES:(j + 1) * LANES]

                def gather(r, carry):
                    kv_ref[r] = stage_ref[pl.ds(r, heads, stride=pitch), :]
                    return carry

                lax.fori_loop(0, rows, gather, 0, unroll=16)

        @pl.when(m == 0)
        def _sample():
            os_ref[...] = compute(as_ref[...].astype(BF16)).astype(os_ref.dtype)

    middle = jnp.logical_and(n > 0, n < nt)
    last_bufs = even_bufs if (nt - 1) % 2 == 0 else odd_bufs

    @pl.when(n == 0)
    def _first_pass():
        cast_into(even_bufs)

    @pl.when(jnp.logical_and(middle, n % 2 == 1))
    def _odd_pass():
        cast_into(odd_bufs)
        multiply_with(even_bufs)

    @pl.when(jnp.logical_and(middle, n % 2 == 0))
    def _even_pass():
        cast_into(even_bufs)
        multiply_with(odd_bufs)

    @pl.when(n == nt)
    def _last_pass():
        multiply_with(last_bufs)


def _kv_plan(layer, prev, bm, tn, mt):
    heads_per_tile = tn // HEAD_DIM
    tiles_per_seg = ATTN_WIDTH // tn
    tiles_per_batch = SEQ // bm
    plan = []
    for g, (window, _) in enumerate(GROUPS):
        keep = min(window, SEQ)
        per_batch = keep < SEQ
        rows = keep if per_batch else bm
        assert rows <= bm
        first_tile = (3 * g + 1) * tiles_per_seg
        n_tiles = 2 * tiles_per_seg
        last_row_block = (BATCH if per_batch else mt) - 1

        def index_map(n, m, first_tile=first_tile, n_tiles=n_tiles, per_batch=per_batch,
                      last_row_block=last_row_block):
            tile = n - 1 - first_tile
            inside = m // tiles_per_batch if per_batch else m
            row_block = jnp.where(tile < 0, 0, jnp.where(tile >= n_tiles, last_row_block, inside))
            tile = jnp.clip(tile, 0, n_tiles - 1)
            return (layer, row_block, tile // tiles_per_seg, tile % tiles_per_seg, 0)

        plan.append(dict(
            first_tile=first_tile, n_tiles=n_tiles, per_batch=per_batch,
            tiles_per_batch=tiles_per_batch, aliased=prev is not None,
            spec=pl.BlockSpec((None, rows, None, heads_per_tile, HEAD_DIM), index_map),
            shape=jax.ShapeDtypeStruct((DEPTH, BATCH * keep, 2, N_HEADS, HEAD_DIM), F32)))
    return plan


def _ws_matmul(a, a_s, weights, *, k, n_out, col_off, bm, tn, out_dtype, swiglu, vmem_mib, name,
               slab_out=False, kv_out=None):
    m_rows = a.shape[0]
    mt = m_rows // bm
    nt = n_out // tn
    ck = k // mt
    assert mt * bm == m_rows and nt * tn == n_out and ck * mt == k and ck % 16 == 0
    coff = col_off // tn
    assert coff * tn == col_off
    ms = a_s.shape[0]

    def a_map(n, m):
        return (jnp.where(n == 0, 0, m), 0)

    def w_map(prefix):
        def f(n, m):
            return prefix + (jnp.where(n == nt, mt - 1, m), jnp.minimum(n, nt - 1) + coff)
        return f

    def os_map(n, m):
        return (0, jnp.maximum(n - 1, 0))

    if slab_out:
        o_spec = pl.BlockSpec((tn // LANES, bm, LANES),
                              lambda n, m: (jnp.maximum(n - 1, 0), jnp.where(n == 0, 0, m), 0))
        o_shape = jax.ShapeDtypeStruct((n_out // LANES, m_rows, LANES), out_dtype)
    else:
        o_spec = pl.BlockSpec((bm, tn),
                              lambda n, m: (jnp.where(n == 0, 0, m), jnp.maximum(n - 1, 0)))
        o_shape = jax.ShapeDtypeStruct((m_rows, n_out), out_dtype)

    in_specs = [pl.BlockSpec((bm, k), a_map),
                pl.BlockSpec((ms, k), lambda n, m: (0, 0))]
    for arr, prefix in weights:
        in_specs.append(pl.BlockSpec((None,) * len(prefix) + (ck, tn), w_map(tuple(prefix))))
    operands = [a, a_s] + [arr for arr, _ in weights]
    out_specs = [o_spec, pl.BlockSpec((ms, tn), os_map)]
    out_shape = [o_shape, jax.ShapeDtypeStruct((ms, n_out), F32)]
    kv_plan, aliases = [], {}
    if kv_out is not None:
        layer, prev = kv_out
        kv_plan = _kv_plan(layer, prev, bm, tn, mt)
        if prev is not None:
            for i, arr in enumerate(prev):
                aliases[len(operands)] = len(out_shape) + i
                operands.append(arr)
                in_specs.append(pl.BlockSpec(memory_space=pl.ANY))
        out_specs += [p["spec"] for p in kv_plan]
        out_shape += [p["shape"] for p in kv_plan]
    kernel_plan = [{key: p[key] for key in ("first_tile", "n_tiles", "per_batch", "tiles_per_batch",
                                            "aliased")} for p in kv_plan]
    scratch_shapes = [pltpu.VMEM((k, tn), BF16) for _ in range(2 * len(weights))]
    if kv_plan:
        scratch_shapes.append(pltpu.VMEM(((tn // HEAD_DIM) * (bm + KV_STAGE_PAD), LANES), F32))
    return pl.pallas_call(
        functools.partial(_ws_matmul_kernel, n_w=len(weights), nt=nt, ck=ck, swiglu=swiglu,
                          slab_out=slab_out, kv_plan=kernel_plan),
        grid=(nt + 1, mt),
        in_specs=in_specs,
        out_specs=out_specs,
        out_shape=out_shape,
        input_output_aliases=aliases,
        scratch_shapes=scratch_shapes,
        compiler_params=_params(vmem_mib, 2),
        name=name,
    )(*operands)


def _t5_bucket(dist):
    dist = np.asarray(dist)
    max_exact = N_BUCKETS // 2
    large = max_exact + (np.log(np.maximum(dist, max_exact) / max_exact)
                         / np.log(MAX_DISTANCE / max_exact)
                         * (N_BUCKETS - max_exact)).astype(np.int32)
    large = np.minimum(large, N_BUCKETS - 1)
    return np.where(dist < max_exact, dist, large).astype(np.int32)


def _group_bias(rel_bias, g, dil):
    buckets = _t5_bucket(np.arange(N_REL + 1) * dil)
    return rel_bias[buckets][:, g * N_HEADS:(g + 1) * N_HEADS].T.astype(F32)


def _neg(*shape):
    return jnp.full(shape, NEG, F32)


def _band_bias(bias_g):
    period = 3 * BAND
    diag = jnp.concatenate([_neg(N_HEADS, BAND - 1), bias_g[:, ::-1],
                            _neg(N_HEADS, period - 2 * BAND)], axis=1)
    flat = jnp.tile(diag, (1, BAND))[:, :BAND * (period - 1)]
    return flat.reshape(N_HEADS, BAND, period - 1)[:, :, BAND - 1:3 * BAND - 1]


def _sample_bias(bias_g, dil):
    rev = bias_g[:, ::-1]
    cached, new = [], []
    for t in range(DEC_SEQ):
        shift = t if dil == 1 else 0
        if t == 0 or dil == 1:
            row = jnp.concatenate([_neg(N_HEADS, shift), rev[:, :N_REL - shift]], axis=1)
            cached.append(row.T)
        cols = []
        for c in range(SAMPLE_PAD):
            valid = c < DEC_SEQ and c <= t and (t - c) % dil == 0
            cols.append(bias_g[:, (t - c) // dil] if valid else _neg(N_HEADS))
        new.append(jnp.stack(cols))
    bc = jnp.stack(cached)
    bn = jnp.stack(new)
    return (jnp.broadcast_to(bc[..., None], bc.shape + (LANES,)),
            jnp.broadcast_to(bn[..., None], bn.shape + (LANES,)))


def _attn_prompt_kernel(q_ref, k_ref, v_ref, b_ref, o_ref, lse_ref, *, hb, dil, unroll):
    hblk = pl.program_id(1)
    nb = SEQ // dil // BAND
    lane = lax.broadcasted_iota(jnp.int32, (BAND, LANES), 1)
    col = lax.broadcasted_iota(jnp.int32, (BAND, 2 * BAND), 1)

    @pl.when(hblk == 0)
    def _init():
        lse_ref[...] = jnp.zeros_like(lse_ref)

    def rows_at(row0):
        if dil == 1:
            return pl.ds(pl.multiple_of(row0, BAND), BAND)
        return pl.ds(row0, BAND, stride=dil)

    def trip(t, carry):
        chains = []
        for u in range(unroll):
            idx = t * unroll + u
            if nb == 1:
                row0 = idx
                mask = None
            else:
                beta = lax.rem(idx, nb)
                row0 = lax.div(idx, nb) + beta * (BAND * dil)
                first = beta == 0
                prev = rows_at(jnp.where(first, row0, row0 - BAND * dil))
                mask = jnp.logical_and(first, col < BAND)
            cur = rows_at(row0)
            for h in range(hb):
                q = q_ref[h, cur, :].astype(BF16)
                k2 = k_ref[h, cur, :].astype(BF16)
                v2 = v_ref[h, cur, :].astype(BF16)
                if nb == 1:
                    bias = b_ref[h][:, BAND:]
                else:
                    bias = b_ref[h]
                    k2 = jnp.concatenate([k_ref[h, prev, :].astype(BF16), k2], axis=0)
                    v2 = jnp.concatenate([v_ref[h, prev, :].astype(BF16), v2], axis=0)
                s = lax.dot_general(q, k2, (((1,), (1,)), ((), ())), preferred_element_type=F32)
                chains.append(dict(u=u, h=h, cur=cur, s=s, bias=bias, mask=mask, v2=v2))
        for c in chains:
            s = c["s"] * ATTN_SCALE + c["bias"]
            if c["mask"] is not None:
                s = jnp.where(c["mask"], NEG, s)
            mx = jnp.max(s, axis=-1, keepdims=True)
            p = jnp.exp(s - mx)
            den = jnp.sum(p, axis=-1, keepdims=True)
            c["p"] = (p / den).astype(BF16)
            c["lse"] = mx + jnp.log(den)
        for c in chains:
            c["o"] = jnp.dot(c["p"], c["v2"], preferred_element_type=F32)
        for u in range(unroll):
            mine = [c for c in chains if c["u"] == u]
            cur = mine[0]["cur"]
            lse_acc = lse_ref[cur, :]
            for c in mine:
                o_ref[c["h"], cur, :] = c["o"]
                lse_acc = jnp.where(lane == hblk * hb + c["h"], c["lse"], lse_acc)
            lse_ref[cur, :] = lse_acc
        return carry

    lax.fori_loop(0, dil * nb // unroll, trip, 0)


def _attn_prompt(qkv, bias_band, g, hb, unroll):
    _, dil = GROUPS[g]

    def slab_map(which):
        first = (3 * g + which) * N_HEADS // hb
        return lambda b, h: (first + h, b, 0)

    blk = (hb, SEQ, LANES)
    return pl.pallas_call(
        functools.partial(_attn_prompt_kernel, hb=hb, dil=dil, unroll=unroll),
        grid=(BATCH, N_HEADS // hb),
        in_specs=[pl.BlockSpec(blk, slab_map(0)),
                  pl.BlockSpec(blk, slab_map(1)),
                  pl.BlockSpec(blk, slab_map(2)),
                  pl.BlockSpec((hb, BAND, 2 * BAND), lambda b, h: (h, 0, 0))],
        out_specs=[pl.BlockSpec(blk, lambda b, h: (h, b, 0)),
                   pl.BlockSpec((SEQ, LANES), lambda b, h: (b, 0))],
        out_shape=[jax.ShapeDtypeStruct((N_HEADS, M_PROMPT, LANES), F32),
                   jax.ShapeDtypeStruct((M_PROMPT, LANES), F32)],
        compiler_params=_params(40, 2),
        name=f"attn_prompt_g{g}",
    )(qkv, qkv, qkv, bias_band)


def _attn_sample_kernel(q_ref, kn_ref, vn_ref, kv_ref, bc_ref, bn_ref, o_ref, lse_ref):
    for ht in range(N_HEADS // SUBLANES):
        hs = slice(ht * SUBLANES, (ht + 1) * SUBLANES)
        vs = slice(N_HEADS + ht * SUBLANES, N_HEADS + (ht + 1) * SUBLANES)
        q = q_ref[hs, :][None]
        s_c = jnp.sum(kv_ref[:, hs, :] * q, axis=-1, keepdims=True)
        s_c = s_c * ATTN_SCALE + bc_ref[:, hs, 0:1]
        s_n = jnp.sum(kn_ref[:, hs, :] * q, axis=-1, keepdims=True)
        s_n = s_n * ATTN_SCALE + bn_ref[:, hs, 0:1]
        mx = jnp.maximum(jnp.max(s_c, axis=0, keepdims=True), jnp.max(s_n, axis=0, keepdims=True))
        p_c = jnp.exp(s_c - mx)
        p_n = jnp.exp(s_n - mx)
        den = jnp.sum(p_c, axis=0, keepdims=True) + jnp.sum(p_n, axis=0, keepdims=True)
        o = (jnp.sum((p_c / den) * kv_ref[:, vs, :], axis=0)
             + jnp.sum((p_n / den) * vn_ref[:, hs, :], axis=0))
        o_ref[hs, :] = o
        lse_ref[hs, :] = jnp.broadcast_to((mx + jnp.log(den))[0], (SUBLANES, LANES))


def _attn_sample(qkv_s, cache, layer, bias_c, bias_n, g):
    _, dil = GROUPS[g]
    cache_r = cache.reshape(DEPTH, DEC_BATCH, N_REL, dil * 2 * N_HEADS, HEAD_DIM)

    def residue(t):
        return t if dil > 1 else 0

    head_blk = (None, None, N_HEADS, HEAD_DIM)
    o, lse = pl.pallas_call(
        _attn_sample_kernel,
        grid=(DEC_BATCH, DEC_SEQ),
        in_specs=[pl.BlockSpec((None, None, None, N_HEADS, HEAD_DIM), lambda b, t: (b, t, 3 * g, 0, 0)),
                  pl.BlockSpec((None, SAMPLE_PAD, None, N_HEADS, HEAD_DIM),
                               lambda b, t: (b, 0, 3 * g + 1, 0, 0)),
                  pl.BlockSpec((None, SAMPLE_PAD, None, N_HEADS, HEAD_DIM),
                               lambda b, t: (b, 0, 3 * g + 2, 0, 0)),
                  pl.BlockSpec((None, None, N_REL, 2 * N_HEADS, HEAD_DIM),
                               lambda b, t: (layer, b, 0, residue(t), 0)),
                  pl.BlockSpec((None, N_REL, N_HEADS, LANES),
                               lambda b, t: (jnp.minimum(t, bias_c.shape[0] - 1), 0, 0, 0)),
                  pl.BlockSpec((None, SAMPLE_PAD, N_HEADS, LANES), lambda b, t: (t, 0, 0, 0))],
        out_specs=[pl.BlockSpec(head_blk, lambda b, t: (b, t, 0, 0)),
                   pl.BlockSpec(head_blk, lambda b, t: (b, t, 0, 0))],
        out_shape=[jax.ShapeDtypeStruct((DEC_BATCH, DEC_SEQ, N_HEADS, HEAD_DIM), F32),
                   jax.ShapeDtypeStruct((DEC_BATCH, DEC_SEQ, N_HEADS, LANES), F32)],
        compiler_params=_params(40, 2),
        name=f"attn_sample_g{g}",
    )(qkv_s, qkv_s, qkv_s, cache_r, bias_c, bias_n)
    o_rows = jnp.pad(o.reshape(DEC_BATCH, DEC_SEQ, ATTN_WIDTH),
                     ((0, 0), (0, SAMPLE_PAD - DEC_SEQ), (0, 0))).reshape(M_SAMPLE, ATTN_WIDTH)
    lse_rows = jnp.pad(lse[..., 0], ((0, 0), (0, SAMPLE_PAD - DEC_SEQ), (0, LANES - N_HEADS)))
    return o_rows, lse_rows.reshape(M_SAMPLE, LANES)


def _mix_kernel(*refs, tr, sample):
    (bg_ref, cg_ref, xc_ref, gc_ref, ga_ref, o0_ref, o1_ref, o2_ref,
     l0_ref, l1_ref, l2_ref, cw_ref) = refs[:12]
    if sample:
        prev_ref, merged_ref, cs_ref = refs[12:]
        n_valid = DEC_SEQ
    else:
        merged_ref, cs_ref, carry_ref = refs[12:]
        prev_ref = carry_ref
        n_valid = tr

        @pl.when(pl.program_id(1) == 0)
        def _zero():
            carry_ref[...] = jnp.zeros_like(carry_ref)

    l0, l1, l2 = l0_ref[...], l1_ref[...], l2_ref[...]
    mx = jnp.maximum(jnp.maximum(l0, l1), l2)
    e0, e1, e2 = jnp.exp(l0 - mx), jnp.exp(l1 - mx), jnp.exp(l2 - mx)
    den = e0 + e1 + e2
    a0, a1, a2 = e0 / den, e1 / den, e2 / den
    row = lax.broadcasted_iota(jnp.int32, (tr, HEAD_DIM), 0)

    for h in range(N_HEADS):
        sl = slice(h * HEAD_DIM, (h + 1) * HEAD_DIM)

        def head_out(o_ref):
            return o_ref[:, sl] if sample else o_ref[h]

        cx = cg_ref[:, sl].astype(F32) * xc_ref[:, sl].astype(F32)
        c0 = prev_ref[0:1, sl]
        c1 = prev_ref[1:2, sl]
        cx1 = jnp.where(row == 0, c1, pltpu.roll(cx, 1, 0))
        cx2 = jnp.where(row == 0, c0, jnp.where(row == 1, c1, pltpu.roll(cx, 2, 0)))
        y = cw_ref[0:1, sl] * cx2
        y = y + cw_ref[1:2, sl] * cx1
        y = y + cw_ref[2:3, sl] * cx
        y_conv = bg_ref[:, sl].astype(F32) * y
        y_attn = (head_out(o0_ref) * a0[:, h:h + 1]
                  + head_out(o1_ref) * a1[:, h:h + 1]
                  + head_out(o2_ref) * a2[:, h:h + 1])
        merged = (jax.nn.sigmoid(gc_ref[:, sl].astype(F32)) * y_conv
                  + jax.nn.sigmoid(ga_ref[:, sl].astype(F32)) * y_attn)
        merged_ref[:, sl] = merged.astype(merged_ref.dtype)
        state = cx[n_valid - 2:n_valid, :]
        cs_ref[:, sl] = state
        if not sample:
            carry_ref[0:2, sl] = state


def _mix_prompt(u_conv, u_gate, outs, lses, conv_w, layer, tr):
    nrt = SEQ // tr

    def col_spec(j):
        return pl.BlockSpec((tr, D_MODEL), lambda b, r: (b * nrt + r, j))

    slab_spec = pl.BlockSpec((N_HEADS, tr, LANES), lambda b, r: (0, b * nrt + r, 0))
    lse_spec = pl.BlockSpec((tr, LANES), lambda b, r: (b * nrt + r, 0))
    return pl.pallas_call(
        functools.partial(_mix_kernel, tr=tr, sample=False),
        grid=(BATCH, nrt),
        in_specs=[col_spec(0), col_spec(1), col_spec(2), col_spec(0), col_spec(1),
                  slab_spec, slab_spec, slab_spec, lse_spec, lse_spec, lse_spec,
                  pl.BlockSpec((None, CONV_WIDTH, D_CONV), lambda b, r: (layer, 0, 0))],
        out_specs=[col_spec(0),
                   pl.BlockSpec((None, CONV_WIDTH - 1, D_CONV), lambda b, r: (b, 0, 0))],
        out_shape=[jax.ShapeDtypeStruct((M_PROMPT, D_MODEL), BF16),
                   jax.ShapeDtypeStruct((BATCH, CONV_WIDTH - 1, D_CONV), F32)],
        scratch_shapes=[pltpu.VMEM((8, D_CONV), F32)],
        compiler_params=_params(48, 2),
        name="mix_prompt",
    )(u_conv, u_conv, u_conv, u_gate, u_gate, *outs, *lses, conv_w)


def _mix_sample(u_conv, u_gate, outs, lses, conv_w, state_conv, layer):
    tr = SAMPLE_PAD

    def col_spec(j):
        return pl.BlockSpec((tr, D_MODEL), lambda b: (b, j))

    lse_spec = pl.BlockSpec((tr, LANES), lambda b: (b, 0))
    state_spec = pl.BlockSpec((None, None, CONV_WIDTH - 1, D_CONV), lambda b: (layer, b, 0, 0))
    return pl.pallas_call(
        functools.partial(_mix_kernel, tr=tr, sample=True),
        grid=(DEC_BATCH,),
        in_specs=[col_spec(0), col_spec(1), col_spec(2), col_spec(0), col_spec(1),
                  col_spec(0), col_spec(0), col_spec(0), lse_spec, lse_spec, lse_spec,
                  pl.BlockSpec((None, CONV_WIDTH, D_CONV), lambda b: (layer, 0, 0)),
                  state_spec],
        out_specs=[col_spec(0),
                   pl.BlockSpec((None, CONV_WIDTH - 1, D_CONV), lambda b: (b, 0, 0))],
        out_shape=[jax.ShapeDtypeStruct((M_SAMPLE, D_MODEL), F32),
                   jax.ShapeDtypeStruct((DEC_BATCH, CONV_WIDTH - 1, D_CONV), F32)],
        compiler_params=_params(40, 1),
        name="mix_sample",
    )(u_conv, u_conv, u_conv, u_gate, u_gate, *outs, *lses, conv_w, state_conv)


ATTN_HB = 4
ATTN_UNROLL = (2, 2, 2)
ROW_TILE = 256
MIX_ROW_TILE = 128


def _ffn(h_p, h_s, w_gate, w_up, w_down, layer, idx):
    act_p, act_s = _ws_matmul(
        h_p, h_s, [(w_gate, (layer, idx)), (w_up, (layer, idx))],
        k=D_MODEL, n_out=D_FF, col_off=0, bm=2048, tn=256, out_dtype=BF16, swiglu=True,
        vmem_mib=56, name="ffn_gate_up")
    return _ws_matmul(
        act_p, act_s, [(w_down, (layer, idx))],
        k=D_FF, n_out=D_MODEL, col_off=0, bm=512, tn=512, out_dtype=F32, swiglu=False,
        vmem_mib=58, name="ffn_down")


def _in_proj(h_p, h_s, w_in, layer, col_off, n_out, out_dtype, slab_out, name, kv_out=None):
    bm = 1024 if kv_out is None else 512
    return _ws_matmul(
        h_p, h_s, [(w_in, (layer,))], k=D_MODEL, n_out=n_out, col_off=col_off, bm=bm, tn=1024,
        out_dtype=out_dtype, swiglu=False, vmem_mib=56, name=name, slab_out=slab_out, kv_out=kv_out)


def kernel(x_prompt, x_sample, state_conv, cache_kv_w128, cache_kv_w512, cache_kv_w2048, rel_bias,
           norm_g, ffn_w_gate, ffn_w_up, ffn_w_down, w_in, conv_w, w_out):
    caches = (cache_kv_w128, cache_kv_w512, cache_kv_w2048)
    xp = x_prompt.reshape(M_PROMPT, D_MODEL)
    xs = jnp.pad(x_sample, ((0, 0), (0, SAMPLE_PAD - DEC_SEQ), (0, 0))).reshape(M_SAMPLE, D_MODEL)

    band_bias, samp_bias = [], []
    for g, (window, dil) in enumerate(GROUPS):
        bias_g = _group_bias(rel_bias, g, dil)
        band_bias.append(_band_bias(bias_g))
        samp_bias.append(_sample_bias(bias_g, dil))

    h_p = _rms_cast(xp, norm_g[0, 0], BF16, ROW_TILE)
    h_s = _rms_cast(xs, norm_g[0, 0], F32, M_SAMPLE)

    conv_p, conv_s = [], []
    kv_p = None
    kv_s = [[] for _ in range(N_GROUPS)]
    for l in range(DEPTH):
        f_p, f_s = _ffn(h_p, h_s, ffn_w_gate, ffn_w_up, ffn_w_down, l, 0)
        xp, h_p = _resid_norm(xp, f_p, norm_g[l, 1], norm_g[l, 2], 0.5, BF16, ROW_TILE)
        xs, h_s = _resid_norm(xs, f_s, norm_g[l, 1], norm_g[l, 2], 0.5, F32, M_SAMPLE)

        conv_in_p, conv_in_s = _in_proj(h_p, h_s, w_in, l, 0, QKV_BASE, BF16, False, "w_in_conv")
        qkv_p, qkv_s, *kv_p = _in_proj(h_p, h_s, w_in, l, QKV_BASE, QKV_WIDTH, F32, True, "w_in_qkv",
                                       kv_out=(l, kv_p))
        gate_p, gate_s = _in_proj(h_p, h_s, w_in, l, GATE_BASE, 2 * D_MODEL, BF16, False, "w_in_gate")

        qkv_sh = qkv_s.reshape(DEC_BATCH, SAMPLE_PAD, 3 * N_GROUPS, N_HEADS, HEAD_DIM)
        outs_p, lses_p, outs_s, lses_s = [], [], [], []
        for g in range(N_GROUPS):
            o, lse = _attn_prompt(qkv_p, band_bias[g], g, ATTN_HB, ATTN_UNROLL[g])
            outs_p.append(o)
            lses_p.append(lse)
            o, lse = _attn_sample(qkv_sh, caches[g], l, samp_bias[g][0], samp_bias[g][1], g)
            outs_s.append(o)
            lses_s.append(lse)
            kv_s[g].append(qkv_sh[:, :DEC_SEQ, 3 * g + 1:3 * g + 3])
        merged_p, cs_p = _mix_prompt(conv_in_p, gate_p, outs_p, lses_p, conv_w, l, MIX_ROW_TILE)
        merged_s, cs_s = _mix_sample(conv_in_s, gate_s, outs_s, lses_s, conv_w, state_conv, l)
        conv_p.append(cs_p)
        conv_s.append(cs_s)
        m_p, m_s = _ws_matmul(
            merged_p, merged_s, [(w_out, (l,))], k=D_MODEL, n_out=D_MODEL, col_off=0, bm=1024, tn=1024,
            out_dtype=F32, swiglu=False, vmem_mib=56, name="w_out")
        xp, h_p = _resid_norm(xp, m_p, norm_g[l, 3], norm_g[l, 4], 1.0, BF16, ROW_TILE)
        xs, h_s = _resid_norm(xs, m_s, norm_g[l, 3], norm_g[l, 4], 1.0, F32, M_SAMPLE)

        f_p, f_s = _ffn(h_p, h_s, ffn_w_gate, ffn_w_up, ffn_w_down, l, 1)
        g_next = norm_g[l + 1, 0] if l + 1 < DEPTH else None
        xp, h_p = _resid_norm(xp, f_p, norm_g[l, 5], g_next, 0.5, BF16, ROW_TILE)
        xs, h_s = _resid_norm(xs, f_s, norm_g[l, 5], g_next, 0.5, F32, M_SAMPLE)

    y_prompt = xp.reshape(BATCH, SEQ, D_MODEL)
    y_sample = xs.reshape(DEC_BATCH, SAMPLE_PAD, D_MODEL)[:, :DEC_SEQ]
    kv128_p, kv512_p, kv2048_p = [
        t.reshape(DEPTH, BATCH, min(window, SEQ), 2, N_HEADS, HEAD_DIM)
        for t, (window, _) in zip(kv_p, GROUPS)]
    kv128_s, kv512_s, kv2048_s = [jnp.stack(t) for t in kv_s]
    return (y_prompt, y_sample, jnp.stack(conv_p), jnp.stack(conv_s),
            kv128_p, kv128_s, kv512_p, kv512_s, kv2048_p, kv2048_s)
```

```python
import functools
import math

import jax
import jax.numpy as jnp
import numpy as np
from jax import lax
from jax.experimental import pallas as pl
from jax.experimental.pallas import tpu as pltpu

D_MODEL = 4096
BATCH = 4
SEQ = 2048
DEPTH = 2
DEC_BATCH = 8
DEC_SEQ = 4

D_CONV = D_MODEL
CONV_WIDTH = 3
HEAD_DIM = 128
N_HEADS = D_MODEL // HEAD_DIM
ATTN_WIDTH = N_HEADS * HEAD_DIM
GROUPS = ((128, 1), (512, 4), (2048, 16))
N_GROUPS = len(GROUPS)
N_REL = 128
BAND = 128
N_BUCKETS = 32
MAX_DISTANCE = 2048
D_FF = 11008
RMS_EPS = 1e-6
NEG = -1e30
ATTN_SCALE = 1.0 / math.sqrt(HEAD_DIM)
QKV_BASE = 3 * D_CONV
QKV_WIDTH = 3 * N_GROUPS * ATTN_WIDTH
GATE_BASE = QKV_BASE + QKV_WIDTH
IN_WIDTH = GATE_BASE + 2 * D_MODEL

LANES = 128
SUBLANES = 8
SAMPLE_PAD = SUBLANES
M_PROMPT = BATCH * SEQ
M_SAMPLE = DEC_BATCH * SAMPLE_PAD
MIB = 1024 * 1024

F32 = jnp.float32
BF16 = jnp.bfloat16


def _params(vmem_mib, n_axes):
    return pltpu.CompilerParams(
        dimension_semantics=("arbitrary",) * n_axes,
        vmem_limit_bytes=int(vmem_mib * MIB))


def _rms(x):
    return x * lax.rsqrt(jnp.mean(x * x, axis=-1, keepdims=True) + RMS_EPS)


def _rms_cast_kernel(x_ref, g_ref, h_ref):
    h_ref[...] = (_rms(x_ref[...]) * g_ref[...]).astype(h_ref.dtype)


def _rms_cast(x, g, out_dtype, tr):
    rows = x.shape[0]
    return pl.pallas_call(
        _rms_cast_kernel,
        grid=(rows // tr,),
        in_specs=[pl.BlockSpec((tr, D_MODEL), lambda i: (i, 0)),
                  pl.BlockSpec((1, D_MODEL), lambda i: (0, 0))],
        out_specs=pl.BlockSpec((tr, D_MODEL), lambda i: (i, 0)),
        out_shape=jax.ShapeDtypeStruct((rows, D_MODEL), out_dtype),
        compiler_params=_params(32, 1),
        name="rms_cast",
    )(x, g.reshape(1, D_MODEL))


def _resid_norm_kernel(x_ref, f_ref, gp_ref, gn_ref, xo_ref, *h_ref, scale):
    x_new = x_ref[...] + scale * (_rms(f_ref[...]) * gp_ref[...])
    xo_ref[...] = x_new
    for ref in h_ref:
        ref[...] = (_rms(x_new) * gn_ref[...]).astype(ref.dtype)


def _resid_norm(x, f, g_post, g_next, scale, h_dtype, tr):
    rows = x.shape[0]
    row_spec = pl.BlockSpec((tr, D_MODEL), lambda i: (i, 0))
    g_spec = pl.BlockSpec((1, D_MODEL), lambda i: (0, 0))
    emit_h = g_next is not None
    outs = pl.pallas_call(
        functools.partial(_resid_norm_kernel, scale=scale),
        grid=(rows // tr,),
        in_specs=[row_spec, row_spec, g_spec, g_spec],
        out_specs=[row_spec, row_spec][:1 + emit_h],
        out_shape=[jax.ShapeDtypeStruct((rows, D_MODEL), F32),
                   jax.ShapeDtypeStruct((rows, D_MODEL), h_dtype)][:1 + emit_h],
        compiler_params=_params(40, 1),
        name="resid_norm",
    )(x, f, g_post.reshape(1, D_MODEL), (g_next if emit_h else g_post).reshape(1, D_MODEL))
    return (outs[0], outs[1]) if emit_h else (outs[0], None)


SWIGLU_ROWS = 512
KV_STAGE_PAD = SUBLANES


def _ws_matmul_kernel(*refs, n_w, nt, ck, swiglu, slab_out, kv_plan):
    a_ref, as_ref = refs[0], refs[1]
    w_refs = refs[2:2 + n_w]
    n_in = 2 + n_w + (len(kv_plan) if kv_plan and kv_plan[0]["aliased"] else 0)
    o_ref, os_ref = refs[n_in], refs[n_in + 1]
    kv_refs = refs[n_in + 2:n_in + 2 + len(kv_plan)]
    scratch = refs[n_in + 2 + len(kv_plan):]
    even_bufs, odd_bufs = scratch[:n_w], scratch[n_w:2 * n_w]
    stage_ref = scratch[2 * n_w] if kv_plan else None
    n = pl.program_id(0)
    m = pl.program_id(1)

    def cast_into(bufs):
        row = pl.multiple_of(m * ck, 16)
        for w_ref, wb in zip(w_refs, bufs):
            wb[pl.ds(row, ck), :] = w_ref[...].astype(BF16)

    def multiply_with(bufs):
        def compute(x):
            ys = [jnp.dot(x, wb[...], preferred_element_type=F32) for wb in bufs]
            if swiglu:
                gate, up = ys
                return gate * jax.nn.sigmoid(gate) * up
            return ys[0]

        if swiglu:
            res = None
            for r0 in range(0, a_ref.shape[0], SWIGLU_ROWS):
                rows = pl.ds(r0, SWIGLU_ROWS)
                o_ref[rows, :] = compute(a_ref[rows, :]).astype(o_ref.dtype)
        else:
            res = compute(a_ref[...]).astype(o_ref.dtype)
            if slab_out:
                for j in range(o_ref.shape[0]):
                    o_ref[j] = res[:, j * LANES:(j + 1) * LANES]
            else:
                o_ref[...] = res

        for kv_ref, plan in zip(kv_refs, kv_plan):
            tile = n - 1 - plan["first_tile"]
            hit = jnp.logical_and(tile >= 0, tile < plan["n_tiles"])
            if plan["per_batch"]:
                hit = jnp.logical_and(hit, m % plan["tiles_per_batch"] == plan["tiles_per_batch"] - 1)
            rows, heads = kv_ref.shape[0], kv_ref.shape[1]
            pitch = res.shape[0] + KV_STAGE_PAD

            @pl.when(hit)
            def _kv(kv_ref=kv_ref, rows=rows, heads=heads, pitch=pitch):
                for j in range(heads):
                    stage_ref[pl.ds(j * pitch, rows), :] = res[res.shape[0] - rows:,
                                                               j * LANES:(j + 1) * LANES]

                def gather(r, carry):
                    kv_ref[r] = stage_ref[pl.ds(r, heads, stride=pitch), :]
                    return carry

                lax.fori_loop(0, rows, gather, 0, unroll=16)

        @pl.when(m == 0)
        def _sample():
            os_ref[...] = compute(as_ref[...].astype(BF16)).astype(os_ref.dtype)

    middle = jnp.logical_and(n > 0, n < nt)
    last_bufs = even_bufs if (nt - 1) % 2 == 0 else odd_bufs

    @pl.when(n == 0)
    def _first_pass():
        cast_into(even_bufs)

    @pl.when(jnp.logical_and(middle, n % 2 == 1))
    def _odd_pass():
        cast_into(odd_bufs)
        multiply_with(even_bufs)

    @pl.when(jnp.logical_and(middle, n % 2 == 0))
    def _even_pass():
        cast_into(even_bufs)
        multiply_with(odd_bufs)

    @pl.when(n == nt)
    def _last_pass():
        multiply_with(last_bufs)


def _kv_plan(layer, prev, bm, tn, mt):
    heads_per_tile = tn // HEAD_DIM
    tiles_per_seg = ATTN_WIDTH // tn
    tiles_per_batch = SEQ // bm
    plan = []
    for g, (window, _) in enumerate(GROUPS):
        keep = min(window, SEQ)
        per_batch = keep < SEQ
        rows = keep if per_batch else bm
        assert rows <= bm
        first_tile = (3 * g + 1) * tiles_per_seg
        n_tiles = 2 * tiles_per_seg
        last_row_block = (BATCH if per_batch else mt) - 1

        def index_map(n, m, first_tile=first_tile, n_tiles=n_tiles, per_batch=per_batch,
                      last_row_block=last_row_block):
            tile = n - 1 - first_tile
            inside = m // tiles_per_batch if per_batch else m
            row_block = jnp.where(tile < 0, 0, jnp.where(tile >= n_tiles, last_row_block, inside))
            tile = jnp.clip(tile, 0, n_tiles - 1)
            return (layer, row_block, tile // tiles_per_seg, tile % tiles_per_seg, 0)

        plan.append(dict(
            first_tile=first_tile, n_tiles=n_tiles, per_batch=per_batch,
            tiles_per_batch=tiles_per_batch, aliased=prev is not None,
            spec=pl.BlockSpec((None, rows, None, heads_per_tile, HEAD_DIM), index_map),
            shape=jax.ShapeDtypeStruct((DEPTH, BATCH * keep, 2, N_HEADS, HEAD_DIM), F32)))
    return plan


def _ws_matmul(a, a_s, weights, *, k, n_out, col_off, bm, tn, out_dtype, swiglu, vmem_mib, name,
               slab_out=False, kv_out=None):
    m_rows = a.shape[0]
    mt = m_rows // bm
    nt = n_out // tn
    ck = k // mt
    assert mt * bm == m_rows and nt * tn == n_out and ck * mt == k and ck % 16 == 0
    coff = col_off // tn
    assert coff * tn == col_off
    ms = a_s.shape[0]

    def a_map(n, m):
        return (jnp.where(n == 0, 0, m), 0)

    def w_map(prefix):
        def f(n, m):
            return prefix + (jnp.where(n == nt, mt - 1, m), jnp.minimum(n, nt - 1) + coff)
        return f

    def os_map(n, m):
        return (0, jnp.maximum(n - 1, 0))

    if slab_out:
        o_spec = pl.BlockSpec((tn // LANES, bm, LANES),
                              lambda n, m: (jnp.maximum(n - 1, 0), jnp.where(n == 0, 0, m), 0))
        o_shape = jax.ShapeDtypeStruct((n_out // LANES, m_rows, LANES), out_dtype)
    else:
        o_spec = pl.BlockSpec((bm, tn),
                              lambda n, m: (jnp.where(n == 0, 0, m), jnp.maximum(n - 1, 0)))
        o_shape = jax.ShapeDtypeStruct((m_rows, n_out), out_dtype)

    in_specs = [pl.BlockSpec((bm, k), a_map),
                pl.BlockSpec((ms, k), lambda n, m: (0, 0))]
    for arr, prefix in weights:
        in_specs.append(pl.BlockSpec((None,) * len(prefix) + (ck, tn), w_map(tuple(prefix))))
    operands = [a, a_s] + [arr for arr, _ in weights]
    out_specs = [o_spec, pl.BlockSpec((ms, tn), os_map)]
    out_shape = [o_shape, jax.ShapeDtypeStruct((ms, n_out), F32)]
    kv_plan, aliases = [], {}
    if kv_out is not None:
        layer, prev = kv_out
        kv_plan = _kv_plan(layer, prev, bm, tn, mt)
        if prev is not None:
            for i, arr in enumerate(prev):
                aliases[len(operands)] = len(out_shape) + i
                operands.append(arr)
                in_specs.append(pl.BlockSpec(memory_space=pl.ANY))
        out_specs += [p["spec"] for p in kv_plan]
        out_shape += [p["shape"] for p in kv_plan]
    kernel_plan = [{key: p[key] for key in ("first_tile", "n_tiles", "per_batch", "tiles_per_batch",
                                            "aliased")} for p in kv_plan]
    scratch_shapes = [pltpu.VMEM((k, tn), BF16) for _ in range(2 * len(weights))]
    if kv_plan:
        scratch_shapes.append(pltpu.VMEM(((tn // HEAD_DIM) * (bm + KV_STAGE_PAD), LANES), F32))
    return pl.pallas_call(
        functools.partial(_ws_matmul_kernel, n_w=len(weights), nt=nt, ck=ck, swiglu=swiglu,
                          slab_out=slab_out, kv_plan=kernel_plan),
        grid=(nt + 1, mt),
        in_specs=in_specs,
        out_specs=out_specs,
        out_shape=out_shape,
        input_output_aliases=aliases,
        scratch_shapes=scratch_shapes,
        compiler_params=_params(vmem_mib, 2),
        name=name,
    )(*operands)


def _t5_bucket(dist):
    dist = np.asarray(dist)
    max_exact = N_BUCKETS // 2
    large = max_exact + (np.log(np.maximum(dist, max_exact) / max_exact)
                         / np.log(MAX_DISTANCE / max_exact)
                         * (N_BUCKETS - max_exact)).astype(np.int32)
    large = np.minimum(large, N_BUCKETS - 1)
    return np.where(dist < max_exact, dist, large).astype(np.int32)


def _group_bias(rel_bias, g, dil):
    buckets = _t5_bucket(np.arange(N_REL + 1) * dil)
    return rel_bias[buckets][:, g * N_HEADS:(g + 1) * N_HEADS].T.astype(F32)


def _neg(*shape):
    return jnp.full(shape, NEG, F32)


def _band_bias(bias_g):
    period = 3 * BAND
    diag = jnp.concatenate([_neg(N_HEADS, BAND - 1), bias_g[:, ::-1],
                            _neg(N_HEADS, period - 2 * BAND)], axis=1)
    flat = jnp.tile(diag, (1, BAND))[:, :BAND * (period - 1)]
    return flat.reshape(N_HEADS, BAND, period - 1)[:, :, BAND - 1:3 * BAND - 1]


def _sample_bias(bias_g, dil):
    rev = bias_g[:, ::-1]
    cached, new = [], []
    for t in range(DEC_SEQ):
        shift = t if dil == 1 else 0
        if t == 0 or dil == 1:
            row = jnp.concatenate([_neg(N_HEADS, shift), rev[:, :N_REL - shift]], axis=1)
            cached.append(row.T)
        cols = []
        for c in range(SAMPLE_PAD):
            valid = c < DEC_SEQ and c <= t and (t - c) % dil == 0
            cols.append(bias_g[:, (t - c) // dil] if valid else _neg(N_HEADS))
        new.append(jnp.stack(cols))
    bc = jnp.stack(cached)
    bn = jnp.stack(new)
    return (jnp.broadcast_to(bc[..., None], bc.shape + (LANES,)),
            jnp.broadcast_to(bn[..., None], bn.shape + (LANES,)))


def _attn_prompt_kernel(q_ref, k_ref, v_ref, b_ref, o_ref, lse_ref, *, hb, dil, unroll):
    hblk = pl.program_id(1)
    nb = SEQ // dil // BAND
    assert nb == 1 or nb % unroll == 0
    lane = lax.broadcasted_iota(jnp.int32, (BAND, LANES), 1)
    col = lax.broadcasted_iota(jnp.int32, (BAND, 2 * BAND), 1)

    @pl.when(hblk == 0)
    def _init():
        lse_ref[...] = jnp.zeros_like(lse_ref)

    def rows_at(row0):
        if dil == 1:
            return pl.ds(pl.multiple_of(row0, BAND), BAND)
        return pl.ds(row0, BAND, stride=dil)

    def trip(t, carry):
        chains = []
        idx0 = t * unroll
        loaded = {}
        for u in range(unroll):
            if nb == 1:
                row0 = idx0 + u
                mask = None
            else:
                beta0 = lax.rem(idx0, nb)
                row0 = lax.div(idx0, nb) + (beta0 + u) * (BAND * dil)
                mask = None
                if u == 0:
                    first = beta0 == 0
                    prev = rows_at(jnp.where(first, row0, row0 - BAND * dil))
                    mask = jnp.logical_and(first, col < BAND)
            cur = rows_at(row0)
            for h in range(hb):
                q = q_ref[h, cur, :].astype(BF16)
                k_cur = k_ref[h, cur, :].astype(BF16)
                v_cur = v_ref[h, cur, :].astype(BF16)
                if nb == 1:
                    bias = b_ref[h][:, BAND:]
                    k2, v2 = k_cur, v_cur
                else:
                    bias = b_ref[h]
                    if u == 0:
                        k_prev = k_ref[h, prev, :].astype(BF16)
                        v_prev = v_ref[h, prev, :].astype(BF16)
                    else:
                        k_prev, v_prev = loaded[h]
                    loaded[h] = (k_cur, v_cur)
                    k2 = jnp.concatenate([k_prev, k_cur], axis=0)
                    v2 = jnp.concatenate([v_prev, v_cur], axis=0)
                s = lax.dot_general(q, k2, (((1,), (1,)), ((), ())), preferred_element_type=F32)
                chains.append(dict(u=u, h=h, cur=cur, s=s, bias=bias, mask=mask, v2=v2))
        for c in chains:
            s = c["s"] * ATTN_SCALE + c["bias"]
            if c["mask"] is not None:
                s = jnp.where(c["mask"], NEG, s)
            mx = jnp.max(s, axis=-1, keepdims=True)
            p = jnp.exp(s - mx)
            den = jnp.sum(p, axis=-1, keepdims=True)
            c["p"] = (p / den).astype(BF16)
            c["lse"] = mx + jnp.log(den)
        for c in chains:
            c["o"] = jnp.dot(c["p"], c["v2"], preferred_element_type=F32)
        for u in range(unroll):
            mine = [c for c in chains if c["u"] == u]
            cur = mine[0]["cur"]
            lse_acc = lse_ref[cur, :]
            for c in mine:
                o_ref[c["h"], cur, :] = c["o"]
                lse_acc = jnp.where(lane == hblk * hb + c["h"], c["lse"], lse_acc)
            lse_ref[cur, :] = lse_acc
        return carry

    lax.fori_loop(0, dil * nb // unroll, trip, 0)


def _attn_prompt(qkv, bias_band, g, hb, unroll):
    _, dil = GROUPS[g]

    def slab_map(which):
        first = (3 * g + which) * N_HEADS // hb
        return lambda b, h: (first + h, b, 0)

    blk = (hb, SEQ, LANES)
    return pl.pallas_call(
        functools.partial(_attn_prompt_kernel, hb=hb, dil=dil, unroll=unroll),
        grid=(BATCH, N_HEADS // hb),
        in_specs=[pl.BlockSpec(blk, slab_map(0)),
                  pl.BlockSpec(blk, slab_map(1)),
                  pl.BlockSpec(blk, slab_map(2)),
                  pl.BlockSpec((hb, BAND, 2 * BAND), lambda b, h: (h, 0, 0))],
        out_specs=[pl.BlockSpec(blk, lambda b, h: (h, b, 0)),
                   pl.BlockSpec((SEQ, LANES), lambda b, h: (b, 0))],
        out_shape=[jax.ShapeDtypeStruct((N_HEADS, M_PROMPT, LANES), F32),
                   jax.ShapeDtypeStruct((M_PROMPT, LANES), F32)],
        compiler_params=_params(40, 2),
        name=f"attn_prompt_g{g}",
    )(qkv, qkv, qkv, bias_band)


def _attn_sample_kernel(q_ref, kn_ref, vn_ref, kv_ref, bc_ref, bn_ref, o_ref, lse_ref):
    for ht in range(N_HEADS // SUBLANES):
        hs = slice(ht * SUBLANES, (ht + 1) * SUBLANES)
        vs = slice(N_HEADS + ht * SUBLANES, N_HEADS + (ht + 1) * SUBLANES)
        q = q_ref[hs, :][None]
        s_c = jnp.sum(kv_ref[:, hs, :] * q, axis=-1, keepdims=True)
        s_c = s_c * ATTN_SCALE + bc_ref[:, hs, 0:1]
        s_n = jnp.sum(kn_ref[:, hs, :] * q, axis=-1, keepdims=True)
        s_n = s_n * ATTN_SCALE + bn_ref[:, hs, 0:1]
        mx = jnp.maximum(jnp.max(s_c, axis=0, keepdims=True), jnp.max(s_n, axis=0, keepdims=True))
        p_c = jnp.exp(s_c - mx)
        p_n = jnp.exp(s_n - mx)
        den = jnp.sum(p_c, axis=0, keepdims=True) + jnp.sum(p_n, axis=0, keepdims=True)
        o = (jnp.sum((p_c / den) * kv_ref[:, vs, :], axis=0)
             + jnp.sum((p_n / den) * vn_ref[:, hs, :], axis=0))
        o_ref[hs, :] = o
        lse_ref[hs, :] = jnp.broadcast_to((mx + jnp.log(den))[0], (SUBLANES, LANES))


def _attn_sample(qkv_s, cache, layer, bias_c, bias_n, g):
    _, dil = GROUPS[g]
    cache_r = cache.reshape(DEPTH, DEC_BATCH, N_REL, dil * 2 * N_HEADS, HEAD_DIM)

    def residue(t):
        return t if dil > 1 else 0

    head_blk = (None, None, N_HEADS, HEAD_DIM)
    o, lse = pl.pallas_call(
        _attn_sample_kernel,
        grid=(DEC_BATCH, DEC_SEQ),
        in_specs=[pl.BlockSpec((None, None, None, N_HEADS, HEAD_DIM), lambda b, t: (b, t, 3 * g, 0, 0)),
                  pl.BlockSpec((None, SAMPLE_PAD, None, N_HEADS, HEAD_DIM),
                               lambda b, t: (b, 0, 3 * g + 1, 0, 0)),
                  pl.BlockSpec((None, SAMPLE_PAD, None, N_HEADS, HEAD_DIM),
                               lambda b, t: (b, 0, 3 * g + 2, 0, 0)),
                  pl.BlockSpec((None, None, N_REL, 2 * N_HEADS, HEAD_DIM),
                               lambda b, t: (layer, b, 0, residue(t), 0)),
                  pl.BlockSpec((None, N_REL, N_HEADS, LANES),
                               lambda b, t: (jnp.minimum(t, bias_c.shape[0] - 1), 0, 0, 0)),
                  pl.BlockSpec((None, SAMPLE_PAD, N_HEADS, LANES), lambda b, t: (t, 0, 0, 0))],
        out_specs=[pl.BlockSpec(head_blk, lambda b, t: (b, t, 0, 0)),
                   pl.BlockSpec(head_blk, lambda b, t: (b, t, 0, 0))],
        out_shape=[jax.ShapeDtypeStruct((DEC_BATCH, DEC_SEQ, N_HEADS, HEAD_DIM), F32),
                   jax.ShapeDtypeStruct((DEC_BATCH, DEC_SEQ, N_HEADS, LANES), F32)],
        compiler_params=_params(40, 2),
        name=f"attn_sample_g{g}",
    )(qkv_s, qkv_s, qkv_s, cache_r, bias_c, bias_n)
    o_rows = jnp.pad(o.reshape(DEC_BATCH, DEC_SEQ, ATTN_WIDTH),
                     ((0, 0), (0, SAMPLE_PAD - DEC_SEQ), (0, 0))).reshape(M_SAMPLE, ATTN_WIDTH)
    lse_rows = jnp.pad(lse[..., 0], ((0, 0), (0, SAMPLE_PAD - DEC_SEQ), (0, LANES - N_HEADS)))
    return o_rows, lse_rows.reshape(M_SAMPLE, LANES)


def _mix_kernel(*refs, tr, sample):
    (bg_ref, cg_ref, xc_ref, gc_ref, ga_ref, o0_ref, o1_ref, o2_ref,
     l0_ref, l1_ref, l2_ref, cw_ref) = refs[:12]
    if sample:
        prev_ref, merged_ref, cs_ref = refs[12:]
        n_valid = DEC_SEQ
    else:
        merged_ref, cs_ref, carry_ref = refs[12:]
        prev_ref = carry_ref
        n_valid = tr

        @pl.when(pl.program_id(1) == 0)
        def _zero():
            carry_ref[...] = jnp.zeros_like(carry_ref)

    l0, l1, l2 = l0_ref[...], l1_ref[...], l2_ref[...]
    mx = jnp.maximum(jnp.maximum(l0, l1), l2)
    e0, e1, e2 = jnp.exp(l0 - mx), jnp.exp(l1 - mx), jnp.exp(l2 - mx)
    den = e0 + e1 + e2
    a0, a1, a2 = e0 / den, e1 / den, e2 / den
    row = lax.broadcasted_iota(jnp.int32, (tr, HEAD_DIM), 0)

    for h in range(N_HEADS):
        sl = slice(h * HEAD_DIM, (h + 1) * HEAD_DIM)

        def head_out(o_ref):
            return o_ref[:, sl] if sample else o_ref[h]

        cx = cg_ref[:, sl].astype(F32) * xc_ref[:, sl].astype(F32)
        c0 = prev_ref[0:1, sl]
        c1 = prev_ref[1:2, sl]
        cx1 = jnp.where(row == 0, c1, pltpu.roll(cx, 1, 0))
        cx2 = jnp.where(row == 0, c0, jnp.where(row == 1, c1, pltpu.roll(cx, 2, 0)))
        y = cw_ref[0:1, sl] * cx2
        y = y + cw_ref[1:2, sl] * cx1
        y = y + cw_ref[2:3, sl] * cx
        y_conv = bg_ref[:, sl].astype(F32) * y
        y_attn = (head_out(o0_ref) * a0[:, h:h + 1]
                  + head_out(o1_ref) * a1[:, h:h + 1]
                  + head_out(o2_ref) * a2[:, h:h + 1])
        merged = (jax.nn.sigmoid(gc_ref[:, sl].astype(F32)) * y_conv
                  + jax.nn.sigmoid(ga_ref[:, sl].astype(F32)) * y_attn)
        merged_ref[:, sl] = merged.astype(merged_ref.dtype)
        state = cx[n_valid - 2:n_valid, :]
        cs_ref[:, sl] = state
        if not sample:
            carry_ref[0:2, sl] = state


def _mix_prompt(u_conv, u_gate, outs, lses, conv_w, layer, tr):
    nrt = SEQ // tr

    def col_spec(j):
        return pl.BlockSpec((tr, D_MODEL), lambda b, r: (b * nrt + r, j))

    slab_spec = pl.BlockSpec((N_HEADS, tr, LANES), lambda b, r: (0, b * nrt + r, 0))
    lse_spec = pl.BlockSpec((tr, LANES), lambda b, r: (b * nrt + r, 0))
    return pl.pallas_call(
        functools.partial(_mix_kernel, tr=tr, sample=False),
        grid=(BATCH, nrt),
        in_specs=[col_spec(0), col_spec(1), col_spec(2), col_spec(0), col_spec(1),
                  slab_spec, slab_spec, slab_spec, lse_spec, lse_spec, lse_spec,
                  pl.BlockSpec((None, CONV_WIDTH, D_CONV), lambda b, r: (layer, 0, 0))],
        out_specs=[col_spec(0),
                   pl.BlockSpec((None, CONV_WIDTH - 1, D_CONV), lambda b, r: (b, 0, 0))],
        out_shape=[jax.ShapeDtypeStruct((M_PROMPT, D_MODEL), BF16),
                   jax.ShapeDtypeStruct((BATCH, CONV_WIDTH - 1, D_CONV), F32)],
        scratch_shapes=[pltpu.VMEM((8, D_CONV), F32)],
        compiler_params=_params(48, 2),
        name="mix_prompt",
    )(u_conv, u_conv, u_conv, u_gate, u_gate, *outs, *lses, conv_w)


def _mix_sample(u_conv, u_gate, outs, lses, conv_w, state_conv, layer):
    tr = SAMPLE_PAD

    def col_spec(j):
        return pl.BlockSpec((tr, D_MODEL), lambda b: (b, j))

    lse_spec = pl.BlockSpec((tr, LANES), lambda b: (b, 0))
    state_spec = pl.BlockSpec((None, None, CONV_WIDTH - 1, D_CONV), lambda b: (layer, b, 0, 0))
    return pl.pallas_call(
        functools.partial(_mix_kernel, tr=tr, sample=True),
        grid=(DEC_BATCH,),
        in_specs=[col_spec(0), col_spec(1), col_spec(2), col_spec(0), col_spec(1),
                  col_spec(0), col_spec(0), col_spec(0), lse_spec, lse_spec, lse_spec,
                  pl.BlockSpec((None, CONV_WIDTH, D_CONV), lambda b: (layer, 0, 0)),
                  state_spec],
        out_specs=[col_spec(0),
                   pl.BlockSpec((None, CONV_WIDTH - 1, D_CONV), lambda b: (b, 0, 0))],
        out_shape=[jax.ShapeDtypeStruct((M_SAMPLE, D_MODEL), F32),
                   jax.ShapeDtypeStruct((DEC_BATCH, CONV_WIDTH - 1, D_CONV), F32)],
        compiler_params=_params(40, 1),
        name="mix_sample",
    )(u_conv, u_conv, u_conv, u_gate, u_gate, *outs, *lses, conv_w, state_conv)


ATTN_HB = 4
ATTN_UNROLL = (4, 4, 4)
ROW_TILE = 256
MIX_ROW_TILE = 128


def _ffn(h_p, h_s, w_gate, w_up, w_down, layer, idx):
    act_p, act_s = _ws_matmul(
        h_p, h_s, [(w_gate, (layer, idx)), (w_up, (layer, idx))],
        k=D_MODEL, n_out=D_FF, col_off=0, bm=2048, tn=256, out_dtype=BF16, swiglu=True,
        vmem_mib=56, name="ffn_gate_up")
    return _ws_matmul(
        act_p, act_s, [(w_down, (layer, idx))],
        k=D_FF, n_out=D_MODEL, col_off=0, bm=512, tn=512, out_dtype=F32, swiglu=False,
        vmem_mib=58, name="ffn_down")


def _in_proj(h_p, h_s, w_in, layer, col_off, n_out, out_dtype, slab_out, name, kv_out=None):
    bm = 1024 if kv_out is None else 512
    return _ws_matmul(
        h_p, h_s, [(w_in, (layer,))], k=D_MODEL, n_out=n_out, col_off=col_off, bm=bm, tn=1024,
        out_dtype=out_dtype, swiglu=False, vmem_mib=56, name=name, slab_out=slab_out, kv_out=kv_out)


def kernel(x_prompt, x_sample, state_conv, cache_kv_w128, cache_kv_w512, cache_kv_w2048, rel_bias,
           norm_g, ffn_w_gate, ffn_w_up, ffn_w_down, w_in, conv_w, w_out):
    caches = (cache_kv_w128, cache_kv_w512, cache_kv_w2048)
    xp = x_prompt.reshape(M_PROMPT, D_MODEL)
    xs = jnp.pad(x_sample, ((0, 0), (0, SAMPLE_PAD - DEC_SEQ), (0, 0))).reshape(M_SAMPLE, D_MODEL)

    band_bias, samp_bias = [], []
    for g, (window, dil) in enumerate(GROUPS):
        bias_g = _group_bias(rel_bias, g, dil)
        band_bias.append(_band_bias(bias_g))
        samp_bias.append(_sample_bias(bias_g, dil))

    h_p = _rms_cast(xp, norm_g[0, 0], BF16, ROW_TILE)
    h_s = _rms_cast(xs, norm_g[0, 0], F32, M_SAMPLE)

    conv_p, conv_s = [], []
    kv_p = None
    kv_s = [[] for _ in range(N_GROUPS)]
    for l in range(DEPTH):
        f_p, f_s = _ffn(h_p, h_s, ffn_w_gate, ffn_w_up, ffn_w_down, l, 0)
        xp, h_p = _resid_norm(xp, f_p, norm_g[l, 1], norm_g[l, 2], 0.5, BF16, ROW_TILE)
        xs, h_s = _resid_norm(xs, f_s, norm_g[l, 1], norm_g[l, 2], 0.5, F32, M_SAMPLE)

        conv_in_p, conv_in_s = _in_proj(h_p, h_s, w_in, l, 0, QKV_BASE, BF16, False, "w_in_conv")
        qkv_p, qkv_s, *kv_p = _in_proj(h_p, h_s, w_in, l, QKV_BASE, QKV_WIDTH, F32, True, "w_in_qkv",
                                       kv_out=(l, kv_p))
        gate_p, gate_s = _in_proj(h_p, h_s, w_in, l, GATE_BASE, 2 * D_MODEL, BF16, False, "w_in_gate")

        qkv_sh = qkv_s.reshape(DEC_BATCH, SAMPLE_PAD, 3 * N_GROUPS, N_HEADS, HEAD_DIM)
        outs_p, lses_p, outs_s, lses_s = [], [], [], []
        for g in range(N_GROUPS):
            o, lse = _attn_prompt(qkv_p, band_bias[g], g, ATTN_HB, ATTN_UNROLL[g])
            outs_p.append(o)
            lses_p.append(lse)
            o, lse = _attn_sample(qkv_sh, caches[g], l, samp_bias[g][0], samp_bias[g][1], g)
            outs_s.append(o)
            lses_s.append(lse)
            kv_s[g].append(qkv_sh[:, :DEC_SEQ, 3 * g + 1:3 * g + 3])
        merged_p, cs_p = _mix_prompt(conv_in_p, gate_p, outs_p, lses_p, conv_w, l, MIX_ROW_TILE)
        merged_s, cs_s = _mix_sample(conv_in_s, gate_s, outs_s, lses_s, conv_w, state_conv, l)
        conv_p.append(cs_p)
        conv_s.append(cs_s)
        m_p, m_s = _ws_matmul(
            merged_p, merged_s, [(w_out, (l,))], k=D_MODEL, n_out=D_MODEL, col_off=0, bm=1024, tn=1024,
            out_dtype=F32, swiglu=False, vmem_mib=56, name="w_out")
        xp, h_p = _resid_norm(xp, m_p, norm_g[l, 3], norm_g[l, 4], 1.0, BF16, ROW_TILE)
        xs, h_s = _resid_norm(xs, m_s, norm_g[l, 3], norm_g[l, 4], 1.0, F32, M_SAMPLE)

        f_p, f_s = _ffn(h_p, h_s, ffn_w_gate, ffn_w_up, ffn_w_down, l, 1)
        g_next = norm_g[l + 1, 0] if l + 1 < DEPTH else None
        xp, h_p = _resid_norm(xp, f_p, norm_g[l, 5], g_next, 0.5, BF16, ROW_TILE)
        xs, h_s = _resid_norm(xs, f_s, norm_g[l, 5], g_next, 0.5, F32, M_SAMPLE)

    y_prompt = xp.reshape(BATCH, SEQ, D_MODEL)
    y_sample = xs.reshape(DEC_BATCH, SAMPLE_PAD, D_MODEL)[:, :DEC_SEQ]
    kv128_p, kv512_p, kv2048_p = [
        t.reshape(DEPTH, BATCH, min(window, SEQ), 2, N_HEADS, HEAD_DIM)
        for t, (window, _) in zip(kv_p, GROUPS)]
    kv128_s, kv512_s, kv2048_s = [jnp.stack(t) for t in kv_s]
    return (y_prompt, y_sample, jnp.stack(conv_p), jnp.stack(conv_s),
            kv128_p, kv128_s, kv512_p, kv512_s, kv2048_p, kv2048_s)
```

```python
import functools
import math

import jax
import jax.numpy as jnp
import numpy as np
from jax import lax
from jax.experimental import pallas as pl
from jax.experimental.pallas import tpu as pltpu

D_MODEL = 4096
BATCH = 4
SEQ = 2048
DEPTH = 2
DEC_BATCH = 8
DEC_SEQ = 4

D_CONV = D_MODEL
CONV_WIDTH = 3
HEAD_DIM = 128
N_HEADS = D_MODEL // HEAD_DIM
ATTN_WIDTH = N_HEADS * HEAD_DIM
GROUPS = ((128, 1), (512, 4), (2048, 16))
N_GROUPS = len(GROUPS)
N_REL = 128
BAND = 128
N_BUCKETS = 32
MAX_DISTANCE = 2048
D_FF = 11008
RMS_EPS = 1e-6
NEG = -1e30
ATTN_SCALE = 1.0 / math.sqrt(HEAD_DIM)
QKV_BASE = 3 * D_CONV
QKV_WIDTH = 3 * N_GROUPS * ATTN_WIDTH
GATE_BASE = QKV_BASE + QKV_WIDTH
IN_WIDTH = GATE_BASE + 2 * D_MODEL

LANES = 128
SUBLANES = 8
SAMPLE_PAD = SUBLANES
M_PROMPT = BATCH * SEQ
M_SAMPLE = DEC_BATCH * SAMPLE_PAD
MIB = 1024 * 1024

F32 = jnp.float32
BF16 = jnp.bfloat16


def _params(vmem_mib, n_axes):
    return pltpu.CompilerParams(
        dimension_semantics=("arbitrary",) * n_axes,
        vmem_limit_bytes=int(vmem_mib * MIB))


def _rms(x):
    return x * lax.rsqrt(jnp.mean(x * x, axis=-1, keepdims=True) + RMS_EPS)


def _rms_cast_kernel(x_ref, g_ref, h_ref):
    h_ref[...] = (_rms(x_ref[...]) * g_ref[...]).astype(h_ref.dtype)


def _rms_cast(x, g, out_dtype, tr):
    rows = x.shape[0]
    return pl.pallas_call(
        _rms_cast_kernel,
        grid=(rows // tr,),
        in_specs=[pl.BlockSpec((tr, D_MODEL), lambda i: (i, 0)),
                  pl.BlockSpec((1, D_MODEL), lambda i: (0, 0))],
        out_specs=pl.BlockSpec((tr, D_MODEL), lambda i: (i, 0)),
        out_shape=jax.ShapeDtypeStruct((rows, D_MODEL), out_dtype),
        compiler_params=_params(32, 1),
        name="rms_cast",
    )(x, g.reshape(1, D_MODEL))


def _resid_norm_kernel(x_ref, f_ref, gp_ref, gn_ref, xo_ref, *h_ref, scale):
    x_new = x_ref[...] + scale * (_rms(f_ref[...]) * gp_ref[...])
    xo_ref[...] = x_new
    for ref in h_ref:
        ref[...] = (_rms(x_new) * gn_ref[...]).astype(ref.dtype)


def _resid_norm(x, f, g_post, g_next, scale, h_dtype, tr):
    rows = x.shape[0]
    row_spec = pl.BlockSpec((tr, D_MODEL), lambda i: (i, 0))
    g_spec = pl.BlockSpec((1, D_MODEL), lambda i: (0, 0))
    emit_h = g_next is not None
    outs = pl.pallas_call(
        functools.partial(_resid_norm_kernel, scale=scale),
        grid=(rows // tr,),
        in_specs=[row_spec, row_spec, g_spec, g_spec],
        out_specs=[row_spec, row_spec][:1 + emit_h],
        out_shape=[jax.ShapeDtypeStruct((rows, D_MODEL), F32),
                   jax.ShapeDtypeStruct((rows, D_MODEL), h_dtype)][:1 + emit_h],
        compiler_params=_params(40, 1),
        name="resid_norm",
    )(x, f, g_post.reshape(1, D_MODEL), (g_next if emit_h else g_post).reshape(1, D_MODEL))
    return (outs[0], outs[1]) if emit_h else (outs[0], None)


SWIGLU_ROWS = 512
KV_STAGE_PAD = SUBLANES


def _ws_matmul_kernel(*refs, n_w, nt, ck, swiglu, slab_out, kv_plan):
    a_ref, as_ref = refs[0], refs[1]
    w_refs = refs[2:2 + n_w]
    n_in = 2 + n_w + (len(kv_plan) if kv_plan and kv_plan[0]["aliased"] else 0)
    o_ref, os_ref = refs[n_in], refs[n_in + 1]
    kv_refs = refs[n_in + 2:n_in + 2 + len(kv_plan)]
    scratch = refs[n_in + 2 + len(kv_plan):]
    even_bufs, odd_bufs = scratch[:n_w], scratch[n_w:2 * n_w]
    stage_ref = scratch[2 * n_w] if kv_plan else None
    n = pl.program_id(0)
    m = pl.program_id(1)

    def cast_into(bufs):
        row = pl.multiple_of(m * ck, 16)
        for w_ref, wb in zip(w_refs, bufs):
            wb[pl.ds(row, ck), :] = w_ref[...].astype(BF16)

    def multiply_with(bufs):
        def compute(x):
            ys = [jnp.dot(x, wb[...], preferred_element_type=F32) for wb in bufs]
            if swiglu:
                gate, up = ys
                return gate * jax.nn.sigmoid(gate) * up
            return ys[0]

        if swiglu:
            res = None
            for r0 in range(0, a_ref.shape[0], SWIGLU_ROWS):
                rows = pl.ds(r0, SWIGLU_ROWS)
                o_ref[rows, :] = compute(a_ref[rows, :]).astype(o_ref.dtype)
        else:
            res = compute(a_ref[...]).astype(o_ref.dtype)
            if slab_out:
                for j in range(o_ref.shape[0]):
                    o_ref[j] = res[:, j * LANES:(j + 1) * LANES]
            else:
                o_ref[...] = res

        for kv_ref, plan in zip(kv_refs, kv_plan):
            tile = n - 1 - plan["first_tile"]
            hit = jnp.logical_and(tile >= 0, tile < plan["n_tiles"])
            if plan["per_batch"]:
                hit = jnp.logical_and(hit, m % plan["tiles_per_batch"] == plan["tiles_per_batch"] - 1)
            rows, heads = kv_ref.shape[0], kv_ref.shape[1]
            pitch = res.shape[0] + KV_STAGE_PAD

            @pl.when(hit)
            def _kv(kv_ref=kv_ref, rows=rows, heads=heads, pitch=pitch):
                for j in range(heads):
                    stage_ref[pl.ds(j * pitch, rows), :] = res[res.shape[0] - rows:,
                                                               j * LANES:(j + 1) * LANES]

                def gather(r, carry):
                    kv_ref[r] = stage_ref[pl.ds(r, heads, stride=pitch), :]
                    return carry

                lax.fori_loop(0, rows, gather, 0, unroll=16)

        @pl.when(m == 0)
        def _sample():
            os_ref[...] = compute(as_ref[...].astype(BF16)).astype(os_ref.dtype)

    middle = jnp.logical_and(n > 0, n < nt)
    last_bufs = even_bufs if (nt - 1) % 2 == 0 else odd_bufs

    @pl.when(n == 0)
    def _first_pass():
        cast_into(even_bufs)

    @pl.when(jnp.logical_and(middle, n % 2 == 1))
    def _odd_pass():
        cast_into(odd_bufs)
        multiply_with(even_bufs)

    @pl.when(jnp.logical_and(middle, n % 2 == 0))
    def _even_pass():
        cast_into(even_bufs)
        multiply_with(odd_bufs)

    @pl.when(n == nt)
    def _last_pass():
        multiply_with(last_bufs)


def _kv_plan(layer, prev, bm, tn, mt):
    heads_per_tile = tn // HEAD_DIM
    tiles_per_seg = ATTN_WIDTH // tn
    tiles_per_batch = SEQ // bm
    plan = []
    for g, (window, _) in enumerate(GROUPS):
        keep = min(window, SEQ)
        per_batch = keep < SEQ
        rows = keep if per_batch else bm
        assert rows <= bm
        first_tile = (3 * g + 1) * tiles_per_seg
        n_tiles = 2 * tiles_per_seg
        last_row_block = (BATCH if per_batch else mt) - 1

        def index_map(n, m, first_tile=first_tile, n_tiles=n_tiles, per_batch=per_batch,
                      last_row_block=last_row_block):
            tile = n - 1 - first_tile
            inside = m // tiles_per_batch if per_batch else m
            row_block = jnp.where(tile < 0, 0, jnp.where(tile >= n_tiles, last_row_block, inside))
            tile = jnp.clip(tile, 0, n_tiles - 1)
            return (layer, row_block, tile // tiles_per_seg, tile % tiles_per_seg, 0)

        plan.append(dict(
            first_tile=first_tile, n_tiles=n_tiles, per_batch=per_batch,
            tiles_per_batch=tiles_per_batch, aliased=prev is not None,
            spec=pl.BlockSpec((None, rows, None, heads_per_tile, HEAD_DIM), index_map),
            shape=jax.ShapeDtypeStruct((DEPTH, BATCH * keep, 2, N_HEADS, HEAD_DIM), F32)))
    return plan


def _ws_matmul(a, a_s, weights, *, k, n_out, col_off, bm, tn, out_dtype, swiglu, vmem_mib, name,
               slab_out=False, kv_out=None):
    m_rows = a.shape[0]
    mt = m_rows // bm
    nt = n_out // tn
    ck = k // mt
    assert mt * bm == m_rows and nt * tn == n_out and ck * mt == k and ck % 16 == 0
    coff = col_off // tn
    assert coff * tn == col_off
    ms = a_s.shape[0]

    def a_map(n, m):
        return (jnp.where(n == 0, 0, m), 0)

    def w_map(prefix):
        def f(n, m):
            return prefix + (jnp.where(n == nt, mt - 1, m), jnp.minimum(n, nt - 1) + coff)
        return f

    def os_map(n, m):
        return (0, jnp.maximum(n - 1, 0))

    if slab_out:
        o_spec = pl.BlockSpec((tn // LANES, bm, LANES),
                              lambda n, m: (jnp.maximum(n - 1, 0), jnp.where(n == 0, 0, m), 0))
        o_shape = jax.ShapeDtypeStruct((n_out // LANES, m_rows, LANES), out_dtype)
    else:
        o_spec = pl.BlockSpec((bm, tn),
                              lambda n, m: (jnp.where(n == 0, 0, m), jnp.maximum(n - 1, 0)))
        o_shape = jax.ShapeDtypeStruct((m_rows, n_out), out_dtype)

    in_specs = [pl.BlockSpec((bm, k), a_map),
                pl.BlockSpec((ms, k), lambda n, m: (0, 0))]
    for arr, prefix in weights:
        in_specs.append(pl.BlockSpec((None,) * len(prefix) + (ck, tn), w_map(tuple(prefix))))
    operands = [a, a_s] + [arr for arr, _ in weights]
    out_specs = [o_spec, pl.BlockSpec((ms, tn), os_map)]
    out_shape = [o_shape, jax.ShapeDtypeStruct((ms, n_out), F32)]
    kv_plan, aliases = [], {}
    if kv_out is not None:
        layer, prev = kv_out
        kv_plan = _kv_plan(layer, prev, bm, tn, mt)
        if prev is not None:
            for i, arr in enumerate(prev):
                aliases[len(operands)] = len(out_shape) + i
                operands.append(arr)
                in_specs.append(pl.BlockSpec(memory_space=pl.ANY))
        out_specs += [p["spec"] for p in kv_plan]
        out_shape += [p["shape"] for p in kv_plan]
    kernel_plan = [{key: p[key] for key in ("first_tile", "n_tiles", "per_batch", "tiles_per_batch",
                                            "aliased")} for p in kv_plan]
    scratch_shapes = [pltpu.VMEM((k, tn), BF16) for _ in range(2 * len(weights))]
    if kv_plan:
        scratch_shapes.append(pltpu.VMEM(((tn // HEAD_DIM) * (bm + KV_STAGE_PAD), LANES), F32))
    return pl.pallas_call(
        functools.partial(_ws_matmul_kernel, n_w=len(weights), nt=nt, ck=ck, swiglu=swiglu,
                          slab_out=slab_out, kv_plan=kernel_plan),
        grid=(nt + 1, mt),
        in_specs=in_specs,
        out_specs=out_specs,
        out_shape=out_shape,
        input_output_aliases=aliases,
        scratch_shapes=scratch_shapes,
        compiler_params=_params(vmem_mib, 2),
        name=name,
    )(*operands)


def _t5_bucket(dist):
    dist = np.asarray(dist)
    max_exact = N_BUCKETS // 2
    large = max_exact + (np.log(np.maximum(dist, max_exact) / max_exact)
                         / np.log(MAX_DISTANCE / max_exact)
                         * (N_BUCKETS - max_exact)).astype(np.int32)
    large = np.minimum(large, N_BUCKETS - 1)
    return np.where(dist < max_exact, dist, large).astype(np.int32)


def _group_bias(rel_bias, g, dil):
    buckets = _t5_bucket(np.arange(N_REL + 1) * dil)
    return rel_bias[buckets][:, g * N_HEADS:(g + 1) * N_HEADS].T.astype(F32)


def _neg(*shape):
    return jnp.full(shape, NEG, F32)


def _band_bias(bias_g):
    period = 3 * BAND
    diag = jnp.concatenate([_neg(N_HEADS, BAND - 1), bias_g[:, ::-1],
                            _neg(N_HEADS, period - 2 * BAND)], axis=1)
    flat = jnp.tile(diag, (1, BAND))[:, :BAND * (period - 1)]
    return flat.reshape(N_HEADS, BAND, period - 1)[:, :, BAND - 1:3 * BAND - 1]


def _sample_bias(bias_g, dil):
    rev = bias_g[:, ::-1]
    cached, new = [], []
    for t in range(DEC_SEQ):
        shift = t if dil == 1 else 0
        if t == 0 or dil == 1:
            row = jnp.concatenate([_neg(N_HEADS, shift), rev[:, :N_REL - shift]], axis=1)
            cached.append(row.T)
        cols = []
        for c in range(SAMPLE_PAD):
            valid = c < DEC_SEQ and c <= t and (t - c) % dil == 0
            cols.append(bias_g[:, (t - c) // dil] if valid else _neg(N_HEADS))
        new.append(jnp.stack(cols))
    bc = jnp.stack(cached)
    bn = jnp.stack(new)
    return (jnp.broadcast_to(bc[..., None], bc.shape + (LANES,)),
            jnp.broadcast_to(bn[..., None], bn.shape + (LANES,)))


def _attn_prompt_kernel(q_ref, k_ref, v_ref, b_ref, o_ref, lse_ref, *, hb, dil, unroll):
    hblk = pl.program_id(1)
    nb = SEQ // dil // BAND
    assert nb == 1 or nb % unroll == 0
    lane = lax.broadcasted_iota(jnp.int32, (BAND, LANES), 1)
    col = lax.broadcasted_iota(jnp.int32, (BAND, 2 * BAND), 1)

    @pl.when(hblk == 0)
    def _init():
        lse_ref[...] = jnp.zeros_like(lse_ref)

    def rows_at(row0):
        if dil == 1:
            return pl.ds(pl.multiple_of(row0, BAND), BAND)
        return pl.ds(row0, BAND, stride=dil)

    def trip(t, carry):
        chains = []
        idx0 = t * unroll
        loaded = {}
        for u in range(unroll):
            if nb == 1:
                row0 = idx0 + u
                mask = None
            else:
                beta0 = lax.rem(idx0, nb)
                row0 = lax.div(idx0, nb) + (beta0 + u) * (BAND * dil)
                mask = None
                if u == 0:
                    first = beta0 == 0
                    prev = rows_at(jnp.where(first, row0, row0 - BAND * dil))
                    mask = jnp.logical_and(first, col < BAND)
            cur = rows_at(row0)
            for h in range(hb):
                q = q_ref[h, cur, :].astype(BF16)
                k_cur = k_ref[h, cur, :].astype(BF16)
                v_cur = v_ref[h, cur, :].astype(BF16)
                if nb == 1:
                    bias = b_ref[h][:, BAND:]
                    k2, v2 = k_cur, v_cur
                else:
                    bias = b_ref[h]
                    if u == 0:
                        k_prev = k_ref[h, prev, :].astype(BF16)
                        v_prev = v_ref[h, prev, :].astype(BF16)
                    else:
                        k_prev, v_prev = loaded[h]
                    loaded[h] = (k_cur, v_cur)
                    k2 = jnp.concatenate([k_prev, k_cur], axis=0)
                    v2 = jnp.concatenate([v_prev, v_cur], axis=0)
                s = lax.dot_general(q, k2, (((1,), (1,)), ((), ())), preferred_element_type=F32)
                chains.append(dict(u=u, h=h, cur=cur, s=s, bias=bias, mask=mask, v2=v2))
        for c in chains:
            s = c["s"] * ATTN_SCALE + c["bias"]
            if c["mask"] is not None:
                s = jnp.where(c["mask"], NEG, s)
            mx = jnp.max(s, axis=-1, keepdims=True)
            p = jnp.exp(s - mx)
            den = jnp.sum(p, axis=-1, keepdims=True)
            c["p"] = (p / den).astype(BF16)
            c["lse"] = mx + jnp.log(den)
        for c in chains:
            c["o"] = jnp.dot(c["p"], c["v2"], preferred_element_type=F32)
        for u in range(unroll):
            mine = [c for c in chains if c["u"] == u]
            cur = mine[0]["cur"]
            lse_acc = lse_ref[cur, :]
            for c in mine:
                lse_acc = jnp.where(lane == hblk * hb + c["h"], c["lse"], lse_acc)
            lse_ref[cur, :] = lse_acc
            for pair in range(hb // 2):
                hi = pltpu.bitcast(mine[2 * pair]["o"].astype(BF16).astype(F32), jnp.uint32)
                lo = pltpu.bitcast(mine[2 * pair + 1]["o"].astype(BF16).astype(F32), jnp.uint32)
                o_ref[pair, cur, :] = hi | lax.shift_right_logical(lo, jnp.uint32(16))
        return carry

    lax.fori_loop(0, dil * nb // unroll, trip, 0)


def _attn_prompt(qkv, bias_band, g, hb, unroll):
    _, dil = GROUPS[g]

    def slab_map(which):
        first = (3 * g + which) * N_HEADS // hb
        return lambda b, h: (first + h, b, 0)

    blk = (hb, SEQ, LANES)
    return pl.pallas_call(
        functools.partial(_attn_prompt_kernel, hb=hb, dil=dil, unroll=unroll),
        grid=(BATCH, N_HEADS // hb),
        in_specs=[pl.BlockSpec(blk, slab_map(0)),
                  pl.BlockSpec(blk, slab_map(1)),
                  pl.BlockSpec(blk, slab_map(2)),
                  pl.BlockSpec((hb, BAND, 2 * BAND), lambda b, h: (h, 0, 0))],
        out_specs=[pl.BlockSpec((hb // 2, SEQ, LANES), lambda b, h: (h, b, 0)),
                   pl.BlockSpec((SEQ, LANES), lambda b, h: (b, 0))],
        out_shape=[jax.ShapeDtypeStruct((N_HEADS // 2, M_PROMPT, LANES), jnp.uint32),
                   jax.ShapeDtypeStruct((M_PROMPT, LANES), F32)],
        compiler_params=_params(40, 2),
        name=f"attn_prompt_g{g}",
    )(qkv, qkv, qkv, bias_band)


def _attn_sample_kernel(q_ref, kn_ref, vn_ref, kv_ref, bc_ref, bn_ref, o_ref, lse_ref):
    for ht in range(N_HEADS // SUBLANES):
        hs = slice(ht * SUBLANES, (ht + 1) * SUBLANES)
        vs = slice(N_HEADS + ht * SUBLANES, N_HEADS + (ht + 1) * SUBLANES)
        q = q_ref[hs, :][None]
        s_c = jnp.sum(kv_ref[:, hs, :] * q, axis=-1, keepdims=True)
        s_c = s_c * ATTN_SCALE + bc_ref[:, hs, 0:1]
        s_n = jnp.sum(kn_ref[:, hs, :] * q, axis=-1, keepdims=True)
        s_n = s_n * ATTN_SCALE + bn_ref[:, hs, 0:1]
        mx = jnp.maximum(jnp.max(s_c, axis=0, keepdims=True), jnp.max(s_n, axis=0, keepdims=True))
        p_c = jnp.exp(s_c - mx)
        p_n = jnp.exp(s_n - mx)
        den = jnp.sum(p_c, axis=0, keepdims=True) + jnp.sum(p_n, axis=0, keepdims=True)
        o = (jnp.sum((p_c / den) * kv_ref[:, vs, :], axis=0)
             + jnp.sum((p_n / den) * vn_ref[:, hs, :], axis=0))
        o_ref[hs, :] = o
        lse_ref[hs, :] = jnp.broadcast_to((mx + jnp.log(den))[0], (SUBLANES, LANES))


def _attn_sample(qkv_s, cache, layer, bias_c, bias_n, g):
    _, dil = GROUPS[g]
    cache_r = cache.reshape(DEPTH, DEC_BATCH, N_REL, dil * 2 * N_HEADS, HEAD_DIM)

    def residue(t):
        return t if dil > 1 else 0

    head_blk = (None, None, N_HEADS, HEAD_DIM)
    o, lse = pl.pallas_call(
        _attn_sample_kernel,
        grid=(DEC_BATCH, DEC_SEQ),
        in_specs=[pl.BlockSpec((None, None, None, N_HEADS, HEAD_DIM), lambda b, t: (b, t, 3 * g, 0, 0)),
                  pl.BlockSpec((None, SAMPLE_PAD, None, N_HEADS, HEAD_DIM),
                               lambda b, t: (b, 0, 3 * g + 1, 0, 0)),
                  pl.BlockSpec((None, SAMPLE_PAD, None, N_HEADS, HEAD_DIM),
                               lambda b, t: (b, 0, 3 * g + 2, 0, 0)),
                  pl.BlockSpec((None, None, N_REL, 2 * N_HEADS, HEAD_DIM),
                               lambda b, t: (layer, b, 0, residue(t), 0)),
                  pl.BlockSpec((None, N_REL, N_HEADS, LANES),
                               lambda b, t: (jnp.minimum(t, bias_c.shape[0] - 1), 0, 0, 0)),
                  pl.BlockSpec((None, SAMPLE_PAD, N_HEADS, LANES), lambda b, t: (t, 0, 0, 0))],
        out_specs=[pl.BlockSpec(head_blk, lambda b, t: (b, t, 0, 0)),
                   pl.BlockSpec(head_blk, lambda b, t: (b, t, 0, 0))],
        out_shape=[jax.ShapeDtypeStruct((DEC_BATCH, DEC_SEQ, N_HEADS, HEAD_DIM), F32),
                   jax.ShapeDtypeStruct((DEC_BATCH, DEC_SEQ, N_HEADS, LANES), F32)],
        compiler_params=_params(40, 2),
        name=f"attn_sample_g{g}",
    )(qkv_s, qkv_s, qkv_s, cache_r, bias_c, bias_n)
    o_rows = jnp.pad(o.reshape(DEC_BATCH, DEC_SEQ, ATTN_WIDTH),
                     ((0, 0), (0, SAMPLE_PAD - DEC_SEQ), (0, 0))).reshape(M_SAMPLE, ATTN_WIDTH)
    lse_rows = jnp.pad(lse[..., 0], ((0, 0), (0, SAMPLE_PAD - DEC_SEQ), (0, LANES - N_HEADS)))
    return o_rows, lse_rows.reshape(M_SAMPLE, LANES)


def _mix_kernel(*refs, tr, sample):
    (bg_ref, cg_ref, xc_ref, gc_ref, ga_ref, o0_ref, o1_ref, o2_ref,
     l0_ref, l1_ref, l2_ref, cw_ref) = refs[:12]
    if sample:
        prev_ref, merged_ref, cs_ref = refs[12:]
        n_valid = DEC_SEQ
    else:
        merged_ref, cs_ref, carry_ref = refs[12:]
        prev_ref = carry_ref
        n_valid = tr

        @pl.when(pl.program_id(1) == 0)
        def _zero():
            carry_ref[...] = jnp.zeros_like(carry_ref)

    l0, l1, l2 = l0_ref[...], l1_ref[...], l2_ref[...]
    mx = jnp.maximum(jnp.maximum(l0, l1), l2)
    e0, e1, e2 = jnp.exp(l0 - mx), jnp.exp(l1 - mx), jnp.exp(l2 - mx)
    den = e0 + e1 + e2
    a0, a1, a2 = e0 / den, e1 / den, e2 / den
    row = lax.broadcasted_iota(jnp.int32, (tr, HEAD_DIM), 0)

    for h in range(N_HEADS):
        sl = slice(h * HEAD_DIM, (h + 1) * HEAD_DIM)

        def head_out(o_ref):
            if sample:
                return o_ref[:, sl]
            word = o_ref[h // 2]
            bits = (word & jnp.uint32(0xFFFF0000)) if h % 2 == 0 else (word << jnp.uint32(16))
            return pltpu.bitcast(bits, F32)

        cx = cg_ref[:, sl].astype(F32) * xc_ref[:, sl].astype(F32)
        c0 = prev_ref[0:1, sl]
        c1 = prev_ref[1:2, sl]
        cx1 = jnp.where(row == 0, c1, pltpu.roll(cx, 1, 0))
        cx2 = jnp.where(row == 0, c0, jnp.where(row == 1, c1, pltpu.roll(cx, 2, 0)))
        y = cw_ref[0:1, sl] * cx2
        y = y + cw_ref[1:2, sl] * cx1
        y = y + cw_ref[2:3, sl] * cx
        y_conv = bg_ref[:, sl].astype(F32) * y
        y_attn = (head_out(o0_ref) * a0[:, h:h + 1]
                  + head_out(o1_ref) * a1[:, h:h + 1]
                  + head_out(o2_ref) * a2[:, h:h + 1])
        merged = (jax.nn.sigmoid(gc_ref[:, sl].astype(F32)) * y_conv
                  + jax.nn.sigmoid(ga_ref[:, sl].astype(F32)) * y_attn)
        merged_ref[:, sl] = merged.astype(merged_ref.dtype)
        state = cx[n_valid - 2:n_valid, :]
        cs_ref[:, sl] = state
        if not sample:
            carry_ref[0:2, sl] = state


def _mix_prompt(u_conv, u_gate, outs, lses, conv_w, layer, tr):
    nrt = SEQ // tr

    def col_spec(j):
        return pl.BlockSpec((tr, D_MODEL), lambda b, r: (b * nrt + r, j))

    slab_spec = pl.BlockSpec((N_HEADS // 2, tr, LANES), lambda b, r: (0, b * nrt + r, 0))
    lse_spec = pl.BlockSpec((tr, LANES), lambda b, r: (b * nrt + r, 0))
    return pl.pallas_call(
        functools.partial(_mix_kernel, tr=tr, sample=False),
        grid=(BATCH, nrt),
        in_specs=[col_spec(0), col_spec(1), col_spec(2), col_spec(0), col_spec(1),
                  slab_spec, slab_spec, slab_spec, lse_spec, lse_spec, lse_spec,
                  pl.BlockSpec((None, CONV_WIDTH, D_CONV), lambda b, r: (layer, 0, 0))],
        out_specs=[col_spec(0),
                   pl.BlockSpec((None, CONV_WIDTH - 1, D_CONV), lambda b, r: (b, 0, 0))],
        out_shape=[jax.ShapeDtypeStruct((M_PROMPT, D_MODEL), BF16),
                   jax.ShapeDtypeStruct((BATCH, CONV_WIDTH - 1, D_CONV), F32)],
        scratch_shapes=[pltpu.VMEM((8, D_CONV), F32)],
        compiler_params=_params(48, 2),
        name="mix_prompt",
    )(u_conv, u_conv, u_conv, u_gate, u_gate, *outs, *lses, conv_w)


def _mix_sample(u_conv, u_gate, outs, lses, conv_w, state_conv, layer):
    tr = SAMPLE_PAD

    def col_spec(j):
        return pl.BlockSpec((tr, D_MODEL), lambda b: (b, j))

    lse_spec = pl.BlockSpec((tr, LANES), lambda b: (b, 0))
    state_spec = pl.BlockSpec((None, None, CONV_WIDTH - 1, D_CONV), lambda b: (layer, b, 0, 0))
    return pl.pallas_call(
        functools.partial(_mix_kernel, tr=tr, sample=True),
        grid=(DEC_BATCH,),
        in_specs=[col_spec(0), col_spec(1), col_spec(2), col_spec(0), col_spec(1),
                  col_spec(0), col_spec(0), col_spec(0), lse_spec, lse_spec, lse_spec,
                  pl.BlockSpec((None, CONV_WIDTH, D_CONV), lambda b: (layer, 0, 0)),
                  state_spec],
        out_specs=[col_spec(0),
                   pl.BlockSpec((None, CONV_WIDTH - 1, D_CONV), lambda b: (b, 0, 0))],
        out_shape=[jax.ShapeDtypeStruct((M_SAMPLE, D_MODEL), F32),
                   jax.ShapeDtypeStruct((DEC_BATCH, CONV_WIDTH - 1, D_CONV), F32)],
        compiler_params=_params(40, 1),
        name="mix_sample",
    )(u_conv, u_conv, u_conv, u_gate, u_gate, *outs, *lses, conv_w, state_conv)


ATTN_HB = 4
ATTN_UNROLL = (4, 4, 4)
ROW_TILE = 256
MIX_ROW_TILE = 128


def _ffn(h_p, h_s, w_gate, w_up, w_down, layer, idx):
    act_p, act_s = _ws_matmul(
        h_p, h_s, [(w_gate, (layer, idx)), (w_up, (layer, idx))],
        k=D_MODEL, n_out=D_FF, col_off=0, bm=2048, tn=256, out_dtype=BF16, swiglu=True,
        vmem_mib=56, name="ffn_gate_up")
    return _ws_matmul(
        act_p, act_s, [(w_down, (layer, idx))],
        k=D_FF, n_out=D_MODEL, col_off=0, bm=512, tn=512, out_dtype=F32, swiglu=False,
        vmem_mib=58, name="ffn_down")


def _in_proj(h_p, h_s, w_in, layer, col_off, n_out, out_dtype, slab_out, name, kv_out=None):
    bm = 1024 if kv_out is None else 512
    return _ws_matmul(
        h_p, h_s, [(w_in, (layer,))], k=D_MODEL, n_out=n_out, col_off=col_off, bm=bm, tn=1024,
        out_dtype=out_dtype, swiglu=False, vmem_mib=56, name=name, slab_out=slab_out, kv_out=kv_out)


def kernel(x_prompt, x_sample, state_conv, cache_kv_w128, cache_kv_w512, cache_kv_w2048, rel_bias,
           norm_g, ffn_w_gate, ffn_w_up, ffn_w_down, w_in, conv_w, w_out):
    caches = (cache_kv_w128, cache_kv_w512, cache_kv_w2048)
    xp = x_prompt.reshape(M_PROMPT, D_MODEL)
    xs = jnp.pad(x_sample, ((0, 0), (0, SAMPLE_PAD - DEC_SEQ), (0, 0))).reshape(M_SAMPLE, D_MODEL)

    band_bias, samp_bias = [], []
    for g, (window, dil) in enumerate(GROUPS):
        bias_g = _group_bias(rel_bias, g, dil)
        band_bias.append(_band_bias(bias_g))
        samp_bias.append(_sample_bias(bias_g, dil))

    h_p = _rms_cast(xp, norm_g[0, 0], BF16, ROW_TILE)
    h_s = _rms_cast(xs, norm_g[0, 0], F32, M_SAMPLE)

    conv_p, conv_s = [], []
    kv_p = None
    kv_s = [[] for _ in range(N_GROUPS)]
    for l in range(DEPTH):
        f_p, f_s = _ffn(h_p, h_s, ffn_w_gate, ffn_w_up, ffn_w_down, l, 0)
        xp, h_p = _resid_norm(xp, f_p, norm_g[l, 1], norm_g[l, 2], 0.5, BF16, ROW_TILE)
        xs, h_s = _resid_norm(xs, f_s, norm_g[l, 1], norm_g[l, 2], 0.5, F32, M_SAMPLE)

        conv_in_p, conv_in_s = _in_proj(h_p, h_s, w_in, l, 0, QKV_BASE, BF16, False, "w_in_conv")
        qkv_p, qkv_s, *kv_p = _in_proj(h_p, h_s, w_in, l, QKV_BASE, QKV_WIDTH, F32, True, "w_in_qkv",
                                       kv_out=(l, kv_p))
        gate_p, gate_s = _in_proj(h_p, h_s, w_in, l, GATE_BASE, 2 * D_MODEL, BF16, False, "w_in_gate")

        qkv_sh = qkv_s.reshape(DEC_BATCH, SAMPLE_PAD, 3 * N_GROUPS, N_HEADS, HEAD_DIM)
        outs_p, lses_p, outs_s, lses_s = [], [], [], []
        for g in range(N_GROUPS):
            o, lse = _attn_prompt(qkv_p, band_bias[g], g, ATTN_HB, ATTN_UNROLL[g])
            outs_p.append(o)
            lses_p.append(lse)
            o, lse = _attn_sample(qkv_sh, caches[g], l, samp_bias[g][0], samp_bias[g][1], g)
            outs_s.append(o)
            lses_s.append(lse)
            kv_s[g].append(qkv_sh[:, :DEC_SEQ, 3 * g + 1:3 * g + 3])
        merged_p, cs_p = _mix_prompt(conv_in_p, gate_p, outs_p, lses_p, conv_w, l, MIX_ROW_TILE)
        merged_s, cs_s = _mix_sample(conv_in_s, gate_s, outs_s, lses_s, conv_w, state_conv, l)
        conv_p.append(cs_p)
        conv_s.append(cs_s)
        m_p, m_s = _ws_matmul(
            merged_p, merged_s, [(w_out, (l,))], k=D_MODEL, n_out=D_MODEL, col_off=0, bm=1024, tn=1024,
            out_dtype=F32, swiglu=False, vmem_mib=56, name="w_out")
        xp, h_p = _resid_norm(xp, m_p, norm_g[l, 3], norm_g[l, 4], 1.0, BF16, ROW_TILE)
        xs, h_s = _resid_norm(xs, m_s, norm_g[l, 3], norm_g[l, 4], 1.0, F32, M_SAMPLE)

        f_p, f_s = _ffn(h_p, h_s, ffn_w_gate, ffn_w_up, ffn_w_down, l, 1)
        g_next = norm_g[l + 1, 0] if l + 1 < DEPTH else None
        xp, h_p = _resid_norm(xp, f_p, norm_g[l, 5], g_next, 0.5, BF16, ROW_TILE)
        xs, h_s = _resid_norm(xs, f_s, norm_g[l, 5], g_next, 0.5, F32, M_SAMPLE)

    y_prompt = xp.reshape(BATCH, SEQ, D_MODEL)
    y_sample = xs.reshape(DEC_BATCH, SAMPLE_PAD, D_MODEL)[:, :DEC_SEQ]
    kv128_p, kv512_p, kv2048_p = [
        t.reshape(DEPTH, BATCH, min(window, SEQ), 2, N_HEADS, HEAD_DIM)
        for t, (window, _) in zip(kv_p, GROUPS)]
    kv128_s, kv512_s, kv2048_s = [jnp.stack(t) for t in kv_s]
    return (y_prompt, y_sample, jnp.stack(conv_p), jnp.stack(conv_s),
            kv128_p, kv128_s, kv512_p, kv512_s, kv2048_p, kv2048_s)
```
